```python
import jax, jax.numpy as jnp
from jax import lax
import numpy as np

D_MODEL = 4096
BATCH = 32
SEQ = 256
DEPTH = 2
DEC_BATCH = 4
DEC_SEQ = 4096
PAST_LEN = 512

GRID_W = 64
N_MIXERS = 2
N_FOURIER_LAYERS = (DEPTH + 1) // 2
N_NA_LAYERS = DEPTH // 2
N_HEADS = 32
HEAD_DIM = D_MODEL // N_HEADS
FOURIER_GROUPS = 8
FOURIER_GROUP_DIM = D_MODEL // FOURIER_GROUPS
NA_ROWS = 8
NA_COLS = 16
N_EXPERT_GROUPS = 4
EXPERTS_PER_GROUP = 8
N_EXPERTS = N_EXPERT_GROUPS * EXPERTS_PER_GROUP
TOP_K_IN_GROUP = 2
EXPERT_FF = 512
MOE_BLOCK = 128
QUERY_BLOCK = 128
NORM_EPS = 1e-6
NEG_INF = -1e30

kernel_name = "hybrid_fnet_natten_hmoe_diffusion_step"


def rmsnorm(x, g):
    x32 = x.astype(jnp.float32)
    y = x32 * lax.rsqrt(jnp.mean(x32 * x32, axis=-1, keepdims=True) + NORM_EPS)
    return (y * g.astype(jnp.float32)).astype(x.dtype)


def ada_params(cond, w_ada, b_ada):
    m = jax.nn.silu(cond) @ w_ada + b_ada
    return jnp.split(m, 6, axis=-1)


def fourier_mix(h, w_in, w_out):
    B, L, D = h.shape
    u = (h @ w_in).reshape(B, L, FOURIER_GROUPS, FOURIER_GROUP_DIM).astype(jnp.float32)
    f = jnp.real(jnp.fft.fft2(u, axes=(1, 3), norm="ortho")).astype(h.dtype)
    return f.reshape(B, L, D) @ w_out


def split_heads(h, w_qkv):
    B, L, _ = h.shape
    qkv = (h @ w_qkv).reshape(B, L, 3, N_HEADS, HEAD_DIM)
    return qkv[:, :, 0] * (HEAD_DIM ** -0.5), qkv[:, :, 1], qkv[:, :, 2]


def context_attention(q, k, v):
    B, L, H, dh = q.shape
    nb = L // QUERY_BLOCK
    qb = q.reshape(B, nb, QUERY_BLOCK, H, dh).transpose(1, 0, 2, 3, 4)

    def block(q_blk):
        s = jnp.einsum('bqhd,bkhd->bhqk', q_blk, k).astype(jnp.float32)
        p = jax.nn.softmax(s, axis=-1).astype(v.dtype)
        return jnp.einsum('bhqk,bkhd->bqhd', p, v)

    o = lax.map(block, qb)
    return o.transpose(1, 0, 2, 3, 4).reshape(B, L, H, dh)


def neighborhood_attention(q, k, v, k_ctx, v_ctx, rpb):
    B, N, H, dh = q.shape
    rows = N // GRID_W
    kr = min(NA_ROWS, rows)
    qg = q.reshape(B, rows, GRID_W, H, dh)
    kg = k.reshape(B, rows, GRID_W, H, dh)
    vg = v.reshape(B, rows, GRID_W, H, dh)
    r_ids = jnp.arange(rows, dtype=jnp.int32)
    row_start = jnp.clip(r_ids - kr // 2, 0, rows - kr)
    cols = jnp.arange(GRID_W, dtype=jnp.int32)
    col_start = jnp.clip(cols - NA_COLS // 2, 0, GRID_W - NA_COLS)
    col_mask = (cols[None, :] >= col_start[:, None]) & (cols[None, :] < col_start[:, None] + NA_COLS)
    dc_idx = jnp.clip(cols[None, :] - cols[:, None] + NA_COLS - 1, 0, 2 * NA_COLS - 2)
    n_lat = kr * GRID_W

    def one_row(args):
        q_r, rs, ri = args
        k_blk = lax.dynamic_slice_in_dim(kg, rs, kr, axis=1)
        v_blk = lax.dynamic_slice_in_dim(vg, rs, kr, axis=1)
        dr_idx = rs + jnp.arange(kr, dtype=jnp.int32) - ri + NA_ROWS - 1
        bias = rpb[:, dr_idx[None, :, None], dc_idx[:, None, :]]
        s_lat = jnp.einsum('bqhd,bjkhd->bhqjk', q_r, k_blk).astype(jnp.float32) + bias.astype(jnp.float32)
        s_lat = jnp.where(col_mask[:, None, :], s_lat, NEG_INF).reshape(B, H, GRID_W, n_lat)
        s_ctx = jnp.einsum('bqhd,blhd->bhql', q_r, k_ctx).astype(jnp.float32)
        p = jax.nn.softmax(jnp.concatenate([s_lat, s_ctx], axis=-1), axis=-1).astype(v.dtype)
        p_lat = p[..., :n_lat].reshape(B, H, GRID_W, kr, GRID_W)
        p_ctx = p[..., n_lat:]
        return (jnp.einsum('bhqjk,bjkhd->bqhd', p_lat, v_blk)
                + jnp.einsum('bhql,blhd->bqhd', p_ctx, v_ctx))

    o = lax.map(one_row, (qg.transpose(1, 0, 2, 3, 4), row_start, r_ids))
    return o.transpose(1, 0, 2, 3, 4).reshape(B, N, H, dh)


def hierarchical_moe(h, w_grp, b_grp, w_exp, b_exp, w_gate, w_up, w_down):
    T, D = h.shape
    grp_p = jax.nn.softmax((h @ w_grp + b_grp).astype(jnp.float32), axis=-1)
    g_idx = jnp.argmax(grp_p, axis=-1).astype(jnp.int32)
    g_w = jnp.take_along_axis(grp_p, g_idx[:, None], axis=-1)
    e_logits = (h @ w_exp + b_exp).astype(jnp.float32).reshape(T, N_EXPERT_GROUPS, EXPERTS_PER_GROUP)
    sel = jnp.take_along_axis(e_logits, g_idx[:, None, None], axis=1)[:, 0]
    e_w, e_loc = lax.top_k(jax.nn.softmax(sel, axis=-1), TOP_K_IN_GROUP)
    e_w = e_w / jnp.sum(e_w, axis=-1, keepdims=True)
    gates = (g_w * e_w).reshape(-1)
    eid = (g_idx[:, None] * EXPERTS_PER_GROUP + e_loc.astype(jnp.int32)).reshape(-1)
    tok = jnp.repeat(jnp.arange(T, dtype=jnp.int32), TOP_K_IN_GROUP)
    A = T * TOP_K_IN_GROUP
    order = jnp.argsort(eid)
    eid_s, tok_s, gate_s = eid[order], tok[order], gates[order]
    counts = jnp.bincount(eid, length=N_EXPERTS)
    starts = jnp.cumsum(counts) - counts
    padded = (counts + MOE_BLOCK - 1) // MOE_BLOCK * MOE_BLOCK
    p_ends = jnp.cumsum(padded)
    p_starts = p_ends - padded
    dest = p_starts[eid_s] + jnp.arange(A, dtype=jnp.int32) - starts[eid_s]
    n_blocks = -(-A // MOE_BLOCK) + N_EXPERTS
    n_slots = n_blocks * MOE_BLOCK
    slot_tok = jnp.full((n_slots,), T, jnp.int32).at[dest].set(tok_s)
    h_pad = jnp.concatenate([h, jnp.zeros((1, D), h.dtype)], axis=0)
    xs = h_pad[slot_tok].reshape(n_blocks, MOE_BLOCK, D)
    blk_e = jnp.minimum(jnp.searchsorted(p_ends, jnp.arange(n_blocks, dtype=jnp.int32) * MOE_BLOCK, side='right'),
                        N_EXPERTS - 1)

    def expert_block(args):
        xb, e = args
        return (jax.nn.silu(xb @ w_gate[e]) * (xb @ w_up[e])) @ w_down[e]

    ys = lax.map(expert_block, (xs, blk_e)).reshape(n_slots, D)
    contrib = ys[dest] * gate_s[:, None].astype(ys.dtype)
    return jax.ops.segment_sum(contrib, tok_s, num_segments=T)


def setup_inputs(seed: int = 0) -> dict:
    key = jax.random.key(seed)
    ks = jax.random.split(key, 24)

    def nrm(k, shape, scale):
        return jax.random.normal(k, shape, jnp.float32) * scale

    D = D_MODEL
    return {
        "x_prompt": nrm(ks[0], (BATCH, SEQ, D), 1.0),
        "x_sample": nrm(ks[1], (DEC_BATCH, DEC_SEQ, D), 1.0),
        "cache_k": nrm(ks[2], (DEC_BATCH, N_NA_LAYERS, PAST_LEN, N_HEADS, HEAD_DIM), 1.0),
        "cache_v": nrm(ks[3], (DEC_BATCH, N_NA_LAYERS, PAST_LEN, N_HEADS, HEAD_DIM), 1.0),
        "c": nrm(ks[4], (DEC_BATCH, D), 1.0),
        "c_ctx": nrm(ks[5], (D,), 1.0),
        "w_ada": nrm(ks[6], (DEPTH, D, 6 * D), 0.5 * D ** -0.5),
        "b_ada": nrm(ks[7], (DEPTH, 6 * D), 0.02),
        "norm_mix_g": 1.0 + nrm(ks[8], (DEPTH, D), 0.02),
        "norm_ffn_g": 1.0 + nrm(ks[9], (DEPTH, D), 0.02),
        "fourier_w_in": nrm(ks[10], (N_FOURIER_LAYERS, D, D), D ** -0.5),
        "fourier_w_out": nrm(ks[11], (N_FOURIER_LAYERS, D, D), D ** -0.5),
        "na_w_qkv": nrm(ks[12], (N_NA_LAYERS, D, 3 * D), D ** -0.5),
        "na_w_out": nrm(ks[13], (N_NA_LAYERS, D, D), D ** -0.5),
        "na_rpb": nrm(ks[14], (N_NA_LAYERS, N_HEADS, 2 * NA_ROWS - 1, 2 * NA_COLS - 1), 0.1),
        "router_grp_w": nrm(ks[15], (DEPTH, D, N_EXPERT_GROUPS), D ** -0.5),
        "router_grp_b": nrm(ks[16], (DEPTH, N_EXPERT_GROUPS), 0.01),
        "router_exp_w": nrm(ks[17], (DEPTH, D, N_EXPERTS), D ** -0.5),
        "router_exp_b": nrm(ks[18], (DEPTH, N_EXPERTS), 0.01),
        "expert_w_gate": nrm(ks[19], (DEPTH, N_EXPERTS, D, EXPERT_FF), D ** -0.5),
        "expert_w_up": nrm(ks[20], (DEPTH, N_EXPERTS, D, EXPERT_FF), D ** -0.5),
        "expert_w_down": nrm(ks[21], (DEPTH, N_EXPERTS, EXPERT_FF, D), EXPERT_FF ** -0.5),
        "final_norm_g": 1.0 + nrm(ks[22], (D,), 0.02),
    }


def reference(x_prompt, x_sample, cache_k, cache_v, c, c_ctx, w_ada, b_ada, norm_mix_g, norm_ffn_g,
              fourier_w_in, fourier_w_out, na_w_qkv, na_w_out, na_rpb,
              router_grp_w, router_grp_b, router_exp_w, router_exp_b,
              expert_w_gate, expert_w_up, expert_w_down, final_norm_g):
    xp, xs = x_prompt, x_sample
    Bp, Lp, D = xp.shape
    Bs, Ls, _ = xs.shape
    new_k_list, new_v_list = [], []
    for i in range(DEPTH):
        j = i // N_MIXERS
        sh_p, sc_p, gt_p, sh2_p, sc2_p, gt2_p = ada_params(c_ctx, w_ada[i], b_ada[i])
        mods_s = ada_params(c, w_ada[i], b_ada[i])
        sh_s, sc_s, gt_s, sh2_s, sc2_s, gt2_s = [m[:, None, :] for m in mods_s]

        hp = rmsnorm(xp, norm_mix_g[i]) * (1 + sc_p) + sh_p
        hs = rmsnorm(xs, norm_mix_g[i]) * (1 + sc_s) + sh_s
        if i % N_MIXERS == 0:
            mp = fourier_mix(hp, fourier_w_in[j], fourier_w_out[j])
            ms = fourier_mix(hs, fourier_w_in[j], fourier_w_out[j])
        else:
            qp, kp, vp = split_heads(hp, na_w_qkv[j])
            qs, ks_, vs = split_heads(hs, na_w_qkv[j])
            new_k_list.append(kp)
            new_v_list.append(vp)
            mp = context_attention(qp, kp, vp).reshape(Bp, Lp, D) @ na_w_out[j]
            ms = neighborhood_attention(qs, ks_, vs, cache_k[:, j], cache_v[:, j], na_rpb[j]).reshape(Bs, Ls, D) @ na_w_out[j]
        xp = xp + gt_p * mp
        xs = xs + gt_s * ms

        hp = rmsnorm(xp, norm_ffn_g[i]) * (1 + sc2_p) + sh2_p
        hs = rmsnorm(xs, norm_ffn_g[i]) * (1 + sc2_s) + sh2_s
        fp = hierarchical_moe(hp.reshape(Bp * Lp, D), router_grp_w[i], router_grp_b[i], router_exp_w[i],
                              router_exp_b[i], expert_w_gate[i], expert_w_up[i], expert_w_down[i])
        fs = hierarchical_moe(hs.reshape(Bs * Ls, D), router_grp_w[i], router_grp_b[i], router_exp_w[i],
                              router_exp_b[i], expert_w_gate[i], expert_w_up[i], expert_w_down[i])
        xp = xp + gt2_p * fp.reshape(Bp, Lp, D)
        xs = xs + gt2_s * fs.reshape(Bs, Ls, D)

    y_prompt = rmsnorm(xp, final_norm_g)
    y_sample = rmsnorm(xs, final_norm_g)
    new_k = jnp.stack(new_k_list, axis=1)
    new_v = jnp.stack(new_v_list, axis=1)
    return (y_prompt, y_sample, new_k, new_v)
```

```python
import collections
import functools

import jax
import jax.numpy as jnp
from jax import lax
from jax.experimental import pallas as pl
from jax.experimental.pallas import tpu as pltpu

BF = jnp.bfloat16
F32 = jnp.float32
I32 = jnp.int32

NORM_EPS = 1e-6
MASKED = -1e30
GRID_W = 64
NA_ROWS = 8
NA_COLS = 16
FOURIER_GROUPS = 8
EXPERTS_PER_GROUP = 8
N_EXPERT_GROUPS = 4
N_EXPERTS = N_EXPERT_GROUPS * EXPERTS_PER_GROUP
N_MODS = 6
COND_ROWS = 8
LANES = 128
SUBLANES = 8
SLAB_PAD = 8
ROUTE_LANES = LANES
V7X_VMEM_LIMIT = 56 * 2**20

Cfg = collections.namedtuple(
    "Cfg", "d n_prompt prompt_len n_latent latent_len heads head_dim past ff moe_block")


def _params(sem, vmem=V7X_VMEM_LIMIT):
    return pltpu.CompilerParams(dimension_semantics=sem, vmem_limit_bytes=vmem)


def _silu(x):
    return x / (1.0 + jnp.exp(-x))


def _group_of_tile(i, tm, cfg):
    start = i * tm
    return jnp.where(start < cfg.n_prompt, 0, 1 + (start - cfg.n_prompt) // cfg.latent_len)


def _mod_spec(cfg, layer, which, tm, width=None, col=False):
    width = cfg.d if width is None else width
    base = layer * COND_ROWS * N_MODS + which
    if col:
        return pl.BlockSpec((1, 1, width), lambda i, j: (base + _group_of_tile(i, tm, cfg) * N_MODS, 0, j))
    return pl.BlockSpec((1, 1, width), lambda i: (base + _group_of_tile(i, tm, cfg) * N_MODS, 0, 0))


def _ada_kernel(c_ref, w_ref, b_ref, o_ref):
    s = _silu(c_ref[...]).astype(BF)
    o_ref[0] = jnp.dot(s, w_ref[0].astype(BF), preferred_element_type=F32) + b_ref[0]


def _ada(cond, w_ada, b_ada):
    depth, d, n = w_ada.shape
    tn = min(512, n)
    return pl.pallas_call(
        _ada_kernel,
        grid=(depth, n // tn),
        in_specs=[pl.BlockSpec((COND_ROWS, d), lambda l, j: (0, 0)),
                  pl.BlockSpec((1, d, tn), lambda l, j: (l, 0, j)),
                  pl.BlockSpec((1, 1, tn), lambda l, j: (l, 0, j))],
        out_specs=pl.BlockSpec((1, COND_ROWS, tn), lambda l, j: (l, 0, j)),
        out_shape=jax.ShapeDtypeStruct((depth, COND_ROWS, n), F32),
        compiler_params=_params(("arbitrary", "arbitrary")),
        name="ada",
    )(cond, w_ada, b_ada.reshape(depth, 1, n))


def _norm_mod(x, g, sc, sh):
    y = x * lax.rsqrt(jnp.mean(x * x, axis=-1, keepdims=True) + NORM_EPS) * g
    return y * (1.0 + sc) + sh


def _nm_kernel(x_ref, g_ref, sc_ref, sh_ref, o_ref):
    o_ref[...] = _norm_mod(x_ref[...], g_ref[...], sc_ref[0], sh_ref[0]).astype(o_ref.dtype)


def _nm(cfg, x, g, mods, layer, sc_k, sh_k):
    t, d = x.shape
    tm = 512
    return pl.pallas_call(
        _nm_kernel,
        grid=(t // tm,),
        in_specs=[pl.BlockSpec((tm, d), lambda i: (i, 0)),
                  pl.BlockSpec((1, d), lambda i: (0, 0)),
                  _mod_spec(cfg, layer, sc_k, tm),
                  _mod_spec(cfg, layer, sh_k, tm)],
        out_specs=pl.BlockSpec((tm, d), lambda i: (i, 0)),
        out_shape=jax.ShapeDtypeStruct((t, d), BF),
        compiler_params=_params(("arbitrary",)),
        name="norm_mod",
    )(x, g.reshape(1, d), mods, mods)


def _fn_kernel(x_ref, g_ref, o_ref):
    x = x_ref[...]
    o_ref[...] = x * lax.rsqrt(jnp.mean(x * x, axis=-1, keepdims=True) + NORM_EPS) * g_ref[...]


def _final_norm(x, g, row0, rows):
    d = x.shape[1]
    tm = 512
    r0 = row0 // tm
    return pl.pallas_call(
        _fn_kernel,
        grid=(rows // tm,),
        in_specs=[pl.BlockSpec((tm, d), lambda i: (r0 + i, 0)),
                  pl.BlockSpec((1, d), lambda i: (0, 0))],
        out_specs=pl.BlockSpec((tm, d), lambda i: (i, 0)),
        out_shape=jax.ShapeDtypeStruct((rows, d), F32),
        compiler_params=_params(("arbitrary",)),
        name="final_norm",
    )(x, g.reshape(1, d))


def _mm_kernel(x_ref, w_ref, o_ref, *, n_scaled, scale):
    acc = jnp.dot(x_ref[...], w_ref[...], preferred_element_type=F32)
    if n_scaled:
        acc = acc * jnp.where(pl.program_id(1) < n_scaled, scale, 1.0)
    o_ref[...] = acc.astype(o_ref.dtype)


def _mm(x, w, *, row0, rows, col0, cols, out_dtype, scaled_cols=0, scale=1.0):
    k = x.shape[1]
    bm, bn = min(1024, rows), min(512, cols)
    r0, c0 = row0 // bm, col0 // bn
    return pl.pallas_call(
        functools.partial(_mm_kernel, n_scaled=scaled_cols // bn, scale=scale),
        grid=(rows // bm, cols // bn),
        in_specs=[pl.BlockSpec((bm, k), lambda i, j: (r0 + i, 0)),
                  pl.BlockSpec((k, bn), lambda i, j: (0, c0 + j))],
        out_specs=pl.BlockSpec((bm, bn), lambda i, j: (i, j)),
        out_shape=jax.ShapeDtypeStruct((rows, cols), out_dtype),
        compiler_params=_params(("arbitrary", "arbitrary")),
        name="matmul",
    )(x, w)


def _mr_kernel(x_ref, w_ref, r_ref, g_ref, o_ref):
    acc = jnp.dot(x_ref[...], w_ref[...], preferred_element_type=F32)
    o_ref[...] = r_ref[...] + g_ref[0] * acc


def _mm_residual(cfg, x, w, res, mods, layer, gate_k):
    t, k = x.shape
    n = w.shape[1]
    bm, bn = 1024, min(512, n)
    return pl.pallas_call(
        _mr_kernel,
        grid=(t // bm, n // bn),
        in_specs=[pl.BlockSpec((bm, k), lambda i, j: (i, 0)),
                  pl.BlockSpec((k, bn), lambda i, j: (0, j)),
                  pl.BlockSpec((bm, bn), lambda i, j: (i, j)),
                  _mod_spec(cfg, layer, gate_k, bm, width=bn, col=True)],
        out_specs=pl.BlockSpec((bm, bn), lambda i, j: (i, j)),
        out_shape=jax.ShapeDtypeStruct((t, n), F32),
        input_output_aliases={2: 0},
        compiler_params=_params(("arbitrary", "arbitrary")),
        name="matmul_residual",
    )(x, w, res, mods)


def _fa_kernel(x_ref, w_ref, cs_ref, o_ref):
    u = jnp.dot(x_ref[...], w_ref[...], preferred_element_type=F32).astype(BF)
    v = jnp.dot(u, cs_ref[...], preferred_element_type=F32)
    n = u.shape[1]
    o_ref[0] = v[:, :n].astype(BF)
    o_ref[1] = v[:, n:].astype(BF)


def _fourier_in(h, w_in, cs):
    t, d = h.shape
    gd = d // FOURIER_GROUPS
    bm = 512
    return pl.pallas_call(
        _fa_kernel,
        grid=(t // bm, FOURIER_GROUPS),
        in_specs=[pl.BlockSpec((bm, d), lambda i, j: (i, 0)),
                  pl.BlockSpec((d, gd), lambda i, j: (0, j)),
                  pl.BlockSpec((gd, 2 * gd), lambda i, j: (0, 0))],
        out_specs=pl.BlockSpec((2, bm, gd), lambda i, j: (0, i, j)),
        out_shape=jax.ShapeDtypeStruct((2, t, d), BF),
        compiler_params=_params(("arbitrary", "arbitrary")),
        name="fourier_in",
    )(h, w_in, cs)


def _fb_kernel(ac_ref, as_ref, vc_ref, vs_ref, *rest):
    o_ref = rest[-1]
    acc = jnp.dot(ac_ref[0], vc_ref[0], preferred_element_type=F32)
    acc = acc + jnp.dot(as_ref[0], vs_ref[0], preferred_element_type=F32)
    o_ref[...] = acc.astype(o_ref.dtype)


def _fourier_seq(a, v, prev, *, row0, n_seq, seq_len, bm, bn):
    _, t, d = v.shape
    rb0, ob0 = row0 // seq_len, row0 // bm
    ni = seq_len // bm
    in_specs = [pl.BlockSpec((1, bm, seq_len), lambda b, j, i: (0, i, 0)),
                pl.BlockSpec((1, bm, seq_len), lambda b, j, i: (1, i, 0)),
                pl.BlockSpec((1, seq_len, bn), lambda b, j, i: (0, rb0 + b, j)),
                pl.BlockSpec((1, seq_len, bn), lambda b, j, i: (1, rb0 + b, j))]
    args = [a, a, v, v]
    aliases = {}
    if prev is not None:
        in_specs.append(pl.BlockSpec(memory_space=pl.ANY))
        args.append(prev)
        aliases = {4: 0}
    return pl.pallas_call(
        _fb_kernel,
        grid=(n_seq, d // bn, ni),
        in_specs=in_specs,
        out_specs=pl.BlockSpec((bm, bn), lambda b, j, i: (ob0 + b * ni + i, j)),
        out_shape=jax.ShapeDtypeStruct((t, d), BF),
        input_output_aliases=aliases,
        compiler_params=_params(("arbitrary", "arbitrary", "arbitrary")),
        name="fourier_seq",
    )(*args)


def _dft_tables(n):
    j = lax.broadcasted_iota(I32, (n, n), 0)
    k = lax.broadcasted_iota(I32, (n, n), 1)
    ang = ((j * k) % n).astype(F32) * (2.0 * jnp.pi / n)
    return jnp.cos(ang), jnp.sin(ang)


def _dft_tables_big(n, r=64):
    a = lax.broadcasted_iota(I32, (r, n), 0)
    k = lax.broadcasted_iota(I32, (r, n), 1)
    ang_hi = ((a * k) % r).astype(F32) * (2.0 * jnp.pi / r)
    ang_lo = ((a * k) % n).astype(F32) * (2.0 * jnp.pi / n)
    ch, sh = jnp.cos(ang_hi)[:, None, :], jnp.sin(ang_hi)[:, None, :]
    cl, sl = jnp.cos(ang_lo)[None, :, :], jnp.sin(ang_lo)[None, :, :]
    cos = (ch * cl - sh * sl).reshape(n, n)
    sin = (sh * cl + ch * sl).reshape(n, n)
    return cos, sin


def _fourier_mix(cfg, h, w_in, w_out_unused=None):
    d = cfg.d
    gd = d // FOURIER_GROUPS
    cc, sc = _dft_tables(gd)
    cs = (jnp.concatenate([cc, sc], axis=1) * gd ** -0.5).astype(BF)
    v = _fourier_in(h, w_in, cs)
    cp, sp = _dft_tables(cfg.prompt_len)
    a_p = (jnp.stack([cp, -sp]) * cfg.prompt_len ** -0.5).astype(BF)
    cl, sl = _dft_tables_big(cfg.latent_len)
    a_l = (jnp.stack([cl, -sl]) * cfg.latent_len ** -0.5).astype(BF)
    f = _fourier_seq(a_p, v, None, row0=0, n_seq=cfg.n_prompt // cfg.prompt_len,
                     seq_len=cfg.prompt_len, bm=cfg.prompt_len, bn=d)
    f = _fourier_seq(a_l, v, f, row0=cfg.n_prompt, n_seq=cfg.n_latent // cfg.latent_len,
                     seq_len=cfg.latent_len, bm=512, bn=512)
    return f


def _softmax_pv(scores, values):
    m = functools.reduce(jnp.maximum, [jnp.max(s, axis=-1, keepdims=True) for s in scores])
    ps = [jnp.exp(s - m) for s in scores]
    l = functools.reduce(jnp.add, [jnp.sum(p, axis=-1, keepdims=True) for p in ps])
    o = functools.reduce(jnp.add, [jnp.dot(p.astype(BF), v, preferred_element_type=F32)
                                   for p, v in zip(ps, values)])
    return o / l


def _qk(q, k):
    return lax.dot_general(q, k, (((1,), (1,)), ((), ())), preferred_element_type=F32)


def _pa_kernel(q_ref, k_ref, v_ref, *rest, heads, dh):
    o_ref = rest[-1]
    for h in range(heads):
        sl = slice(h * dh, (h + 1) * dh)
        s = _qk(q_ref[:, sl], k_ref[:, sl].astype(BF))
        o_ref[:, sl] = _softmax_pv([s], [v_ref[:, sl].astype(BF)]).astype(o_ref.dtype)


def _prompt_attention(cfg, q, k, v, total_rows):
    lp, d = cfg.prompt_len, cfg.d
    hb = min(8, cfg.heads)
    w = hb * cfg.head_dim
    spec = pl.BlockSpec((lp, w), lambda b, g: (b, g))
    return pl.pallas_call(
        functools.partial(_pa_kernel, heads=hb, dh=cfg.head_dim),
        grid=(cfg.n_prompt // lp, d // w),
        in_specs=[spec, spec, spec],
        out_specs=spec,
        out_shape=jax.ShapeDtypeStruct((total_rows, d), BF),
        compiler_params=_params(("arbitrary", "arbitrary")),
        name="prompt_attention",
    )(q, k, v)


def _rx_kernel(r_ref, o_ref):
    r = r_ref[...]
    hi = r.astype(BF)
    r1 = r - hi.astype(F32)
    mid = r1.astype(BF)
    lo = (r1 - mid.astype(F32)).astype(BF)
    n = o_ref.shape[1]
    idx = lax.broadcasted_iota(I32, (ROUTE_LANES, n), 1)
    row = lax.broadcasted_iota(I32, (ROUTE_LANES, n), 0)
    qc, kc = idx >> 7, idx & (GRID_W - 1)
    dc = jnp.clip(kc - qc + NA_COLS - 1, 0, 2 * NA_COLS - 2)
    onehot = jnp.where(row == dc, 1.0, 0.0).astype(BF)
    t = (jnp.dot(hi, onehot, preferred_element_type=F32)
         + jnp.dot(mid, onehot, preferred_element_type=F32)
         + jnp.dot(lo, onehot, preferred_element_type=F32))
    col = lax.broadcasted_iota(I32, (1, n), 1)
    qc1, kc1 = col >> 7, col & (GRID_W - 1)
    c0 = jnp.clip(qc1 - NA_COLS // 2, 0, GRID_W - NA_COLS)
    o_ref[...] = jnp.where((kc1 >= c0) & (kc1 < c0 + NA_COLS), t, MASKED)


def _expand_rpb(rpb):
    heads, nr, nc = rpb.shape
    rows = heads * nr
    tr = rows // 4
    r2 = jnp.pad(rpb.reshape(rows, nc), ((0, 0), (0, ROUTE_LANES - nc)))
    out = pl.pallas_call(
        _rx_kernel,
        grid=(rows // tr,),
        in_specs=[pl.BlockSpec((tr, ROUTE_LANES), lambda i: (i, 0))],
        out_specs=pl.BlockSpec((tr, GRID_W * 2 * GRID_W), lambda i: (i, 0)),
        out_shape=jax.ShapeDtypeStruct((rows, GRID_W * 2 * GRID_W), F32),
        compiler_params=_params(("arbitrary",)),
        name="rpb_expand",
    )(r2)
    return out.reshape(heads, nr, GRID_W, 2 * GRID_W)


NA_QROWS = 4
NA_KROWS = NA_QROWS + NA_ROWS
NA_VARIANTS = ((0, 0), (2 * NA_QROWS, NA_QROWS), (GRID_W - NA_QROWS, GRID_W - NA_KROWS))


def _na_kernel(q_ref, k_ref, v_ref, kc_ref, vc_ref, t_ref, *rest):
    o_ref, bias_ref = rest[-2], rest[-1]
    w = GRID_W

    @pl.when(pl.program_id(1) == 0)
    def _build_bias():
        left = lax.broadcasted_iota(I32, (w, 2 * w), 1) < w
        masked = jnp.full((w, 2 * w), MASKED, F32)

        def table(r, kr):
            rs = min(max(r - NA_ROWS // 2, 0), w - NA_ROWS)
            return t_ref[0, kr - r + NA_ROWS - 1] if rs <= kr < rs + NA_ROWS else masked

        for var, (r0, k0) in enumerate(NA_VARIANTS):
            for i in range(NA_QROWS):
                for j in range(0, NA_KROWS, 2):
                    pair = jnp.where(left, table(r0 + i, k0 + j), table(r0 + i, k0 + j + 1))
                    bias_ref[var, i * w:(i + 1) * w, j * w:(j + 2) * w] = pair

    kc = kc_ref[0].astype(BF)
    vc = vc_ref[0].astype(BF)
    nq = NA_QROWS * w
    nk = NA_KROWS * w
    n_blocks = w // NA_QROWS

    def block(qb, carry):
        var = jnp.where(qb == 0, 0, jnp.where(qb == n_blocks - 1, 2, 1))
        k0 = jnp.clip(qb * NA_QROWS - NA_ROWS // 2, 0, w - NA_KROWS)
        q0 = pl.multiple_of(qb * nq, nq)
        ks = pl.multiple_of(k0 * w, w)
        q = q_ref[pl.ds(q0, nq), :]
        s_lat = _qk(q, k_ref[pl.ds(ks, nk), :]) + bias_ref[var]
        s_ctx = _qk(q, kc)
        o = _softmax_pv([s_lat, s_ctx], [v_ref[pl.ds(ks, nk), :], vc])
        o_ref[pl.ds(q0, nq), :] = o.astype(o_ref.dtype)
        return carry

    lax.fori_loop(0, n_blocks, block, 0)


def _latent_attention(cfg, qkv, cache_k, cache_v, t_bias, prev):
    d, dh, heads, ls = cfg.d, cfg.head_dim, cfg.heads, cfg.latent_len
    nb = cfg.n_latent // ls
    rb0 = cfg.n_prompt // ls
    w = GRID_W
    return pl.pallas_call(
        _na_kernel,
        grid=(heads, nb),
        in_specs=[pl.BlockSpec((ls, dh), lambda h, b: (b, h)),
                  pl.BlockSpec((ls, dh), lambda h, b: (b, heads + h)),
                  pl.BlockSpec((ls, dh), lambda h, b: (b, 2 * heads + h)),
                  pl.BlockSpec((1, cfg.past, dh), lambda h, b: (b, 0, h)),
                  pl.BlockSpec((1, cfg.past, dh), lambda h, b: (b, 0, h)),
                  pl.BlockSpec((1, 2 * NA_ROWS - 1, w, 2 * w), lambda h, b: (h, 0, 0, 0)),
                  pl.BlockSpec(memory_space=pl.ANY)],
        out_specs=pl.BlockSpec((ls, dh), lambda h, b: (rb0 + b, h)),
        out_shape=jax.ShapeDtypeStruct(prev.shape, BF),
        scratch_shapes=[pltpu.VMEM((len(NA_VARIANTS), NA_QROWS * w, NA_KROWS * w), F32)],
        input_output_aliases={6: 0},
        compiler_params=_params(("arbitrary", "arbitrary")),
        name="latent_attention",
    )(qkv, qkv, qkv, cache_k, cache_v, t_bias, prev)


def _slab(d):
    rows = d // LANES
    return rows, rows + SLAB_PAD


def _store_slabs(ref, lead, x):
    n, d = x.shape
    rows, pitch = _slab(d)
    for s in range(rows):
        ref[lead + (pl.ds(s, n, stride=pitch), slice(None))] = x[:, s * LANES:(s + 1) * LANES]


def _load_slab_piece(ref, lead, n, pitch, s):
    return ref[lead + (pl.ds(s, n, stride=pitch), slice(None))]


def _nmr_kernel(x_ref, g_ref, sc_ref, sh_ref, wr_ref, br_ref, hs_ref, ri_ref, rg_ref, cnt_ref, base_ref):
    tm, d = x_ref.shape

    @pl.when(pl.program_id(0) == 0)
    def _init():
        base_ref[...] = jnp.zeros_like(base_ref)

    h = _norm_mod(x_ref[...], g_ref[...], sc_ref[0], sh_ref[0])
    _store_slabs(hs_ref, (), h)

    h_hi = h.astype(BF)
    h_lo = (h - h_hi.astype(F32)).astype(BF)
    wr = wr_ref[...]
    w_hi = wr.astype(BF)
    w_lo = (wr - w_hi.astype(F32)).astype(BF)
    logits = (jnp.dot(h_hi, w_hi, preferred_element_type=F32)
              + jnp.dot(h_lo, w_hi, preferred_element_type=F32)
              + jnp.dot(h_hi, w_lo, preferred_element_type=F32)) + br_ref[...]

    lane = lax.broadcasted_iota(I32, (tm, ROUTE_LANES), 1)
    lane_f = lane.astype(F32)

    def first_argmax(v):
        m = jnp.max(v, axis=-1, keepdims=True)
        first = jnp.min(jnp.where(v == m, lane_f, float(ROUTE_LANES)), axis=-1, keepdims=True)
        return m, first.astype(I32)

    gl = jnp.where(lane < N_EXPERT_GROUPS, logits, MASKED)
    gmax, gi = first_argmax(gl)
    g_w = 1.0 / jnp.sum(jnp.exp(gl - gmax), axis=-1, keepdims=True)
    e0 = N_EXPERT_GROUPS + gi * EXPERTS_PER_GROUP
    el = jnp.where((lane >= e0) & (lane < e0 + EXPERTS_PER_GROUP), logits, MASKED)
    m1, i1 = first_argmax(el)
    m2, i2 = first_argmax(jnp.where(lane == i1, MASKED, el))
    tt = jnp.exp(m2 - m1)
    w1 = 1.0 / (1.0 + tt)
    eid0, eid1 = i1 - N_EXPERT_GROUPS, i2 - N_EXPERT_GROUPS

    oh = jnp.where((lane == eid0) | (lane == eid1), 1.0, 0.0)
    row = lax.broadcasted_iota(I32, (tm, tm), 0)
    colt = lax.broadcasted_iota(I32, (tm, tm), 1)
    earlier = jnp.where(row > colt, 1.0, 0.0).astype(BF)
    before = jnp.dot(earlier, oh.astype(BF), preferred_element_type=F32) + base_ref[0:1, :]
    rank0 = jnp.sum(jnp.where(lane == eid0, before, 0.0), axis=-1, keepdims=True).astype(I32)
    rank1 = jnp.sum(jnp.where(lane == eid1, before, 0.0), axis=-1, keepdims=True).astype(I32)
    base_ref[...] = base_ref[...] + jnp.sum(oh, axis=0, keepdims=True)

    ri_ref[...] = jnp.where(lane == 0, eid0, jnp.where(lane == 1, eid1,
                            jnp.where(lane == 2, rank0, jnp.where(lane == 3, rank1, 0))))
    rg_ref[...] = jnp.where(lane == 0, g_w * w1, jnp.where(lane == 1, g_w * (tt * w1), 0.0))
    cnt_ref[...] = base_ref[...]


def _norm_route(cfg, x, g, mods, layer, w_route, b_route):
    t, d = x.shape
    tm = 256
    _, pitch = _slab(d)
    return pl.pallas_call(
        _nmr_kernel,
        grid=(t // tm,),
        in_specs=[pl.BlockSpec((tm, d), lambda i: (i, 0)),
                  pl.BlockSpec((1, d), lambda i: (0, 0)),
                  _mod_spec(cfg, layer, 4, tm),
                  _mod_spec(cfg, layer, 3, tm),
                  pl.BlockSpec((d, ROUTE_LANES), lambda i: (0, 0)),
                  pl.BlockSpec((1, ROUTE_LANES), lambda i: (0, 0))],
        out_specs=[pl.BlockSpec((tm * pitch, LANES), lambda i: (i, 0)),
                   pl.BlockSpec((tm, ROUTE_LANES), lambda i: (i, 0)),
                   pl.BlockSpec((tm, ROUTE_LANES), lambda i: (i, 0)),
                   pl.BlockSpec((SUBLANES, ROUTE_LANES), lambda i: (0, 0))],
        out_shape=[jax.ShapeDtypeStruct((t * pitch, LANES), F32),
                   jax.ShapeDtypeStruct((t, ROUTE_LANES), I32),
                   jax.ShapeDtypeStruct((t, ROUTE_LANES), F32),
                   jax.ShapeDtypeStruct((SUBLANES, ROUTE_LANES), F32)],
        scratch_shapes=[pltpu.VMEM((SUBLANES, ROUTE_LANES), F32)],
        compiler_params=_params(("arbitrary",)),
        name="norm_route",
    )(x, g.reshape(1, d), mods, mods, w_route, b_route)


DISPATCH_TOKENS = 1024


def _token_copies(dest_ref, sem, n, make):
    def issue(tk, carry):
        make(tk, 0, dest_ref[0, 0, 2 * tk]).start()
        make(tk, 1, dest_ref[0, 0, 2 * tk + 1]).start()
        return carry
    lax.fori_loop(0, n, issue, 0)


def _wait_token_copies(ref, sem, n):
    rows = pl.ds(0, ref.shape[-2] - SLAB_PAD)
    pltpu.make_async_copy(ref.at[pl.ds(0, 2 * n), rows], ref.at[pl.ds(0, 2 * n), rows], sem).wait()


def _ds_kernel(dest_ref, hs_ref, init_ref, xs_ref, sem):
    del init_ref
    n = DISPATCH_TOKENS
    t0 = pl.program_id(0) * n
    rows = pl.ds(0, hs_ref.shape[-2] - SLAB_PAD)
    _token_copies(dest_ref, sem, n,
                  lambda tk, k, dst: pltpu.make_async_copy(hs_ref.at[t0 + tk, rows], xs_ref.at[dst, rows], sem))
    _wait_token_copies(xs_ref, sem, n)


def _dispatch(hs, dest, n_slots):
    t, sr, sw = hs.shape
    n = DISPATCH_TOKENS
    return pl.pallas_call(
        _ds_kernel,
        grid=(t // n,),
        in_specs=[pl.BlockSpec((1, 1, 2 * n), lambda i: (i, 0, 0), memory_space=pltpu.SMEM),
                  pl.BlockSpec(memory_space=pl.ANY),
                  pl.BlockSpec(memory_space=pl.ANY)],
        out_specs=pl.BlockSpec(memory_space=pl.ANY),
        out_shape=jax.ShapeDtypeStruct((n_slots, sr, sw), F32),
        scratch_shapes=[pltpu.SemaphoreType.DMA],
        input_output_aliases={2: 0},
        compiler_params=_params(("arbitrary",)),
        name="moe_dispatch",
    )(dest.reshape(t // n, 1, 2 * n), hs, jnp.zeros((n_slots, sr, sw), F32))


def _rt_kernel(dest_ref, ys_ref, y2_ref, sem):
    n = DISPATCH_TOKENS
    t0 = pl.program_id(0) * n
    rows = pl.ds(0, ys_ref.shape[-2] - SLAB_PAD)
    _token_copies(dest_ref, sem, n,
                  lambda tk, k, src: pltpu.make_async_copy(ys_ref.at[src, rows], y2_ref.at[k, t0 + tk, rows], sem))
    _wait_token_copies(ys_ref, sem, n)


def _gather_back(ys, dest, t):
    _, sr, sw = ys.shape
    n = DISPATCH_TOKENS
    return pl.pallas_call(
        _rt_kernel,
        grid=(t // n,),
        in_specs=[pl.BlockSpec((1, 1, 2 * n), lambda i: (i, 0, 0), memory_space=pltpu.SMEM),
                  pl.BlockSpec(memory_space=pl.ANY)],
        out_specs=pl.BlockSpec(memory_space=pl.ANY),
        out_shape=jax.ShapeDtypeStruct((2, t, sr, sw), F32),
        scratch_shapes=[pltpu.SemaphoreType.DMA],
        compiler_params=_params(("arbitrary",)),
        name="moe_gather_back",
    )(dest.reshape(t // n, 1, 2 * n), ys)


def _ex_kernel(blk_e_ref, n_used_ref, x_ref, wg_ref, wu_ref, wd_ref, o_ref, xb_ref):
    del blk_e_ref
    bm, d = xb_ref.shape
    rows, pitch = _slab(d)

    @pl.when(pl.program_id(0) < n_used_ref[0])
    def _compute():
        for s in range(rows):
            xb_ref[:, s * LANES:(s + 1) * LANES] = _load_slab_piece(x_ref, (), bm, pitch, s).astype(BF)
        x = xb_ref[...]
        g = jnp.dot(x, wg_ref[0], preferred_element_type=F32)
        u = jnp.dot(x, wu_ref[0], preferred_element_type=F32)
        y = jnp.dot((_silu(g) * u).astype(BF), wd_ref[0], preferred_element_type=F32)
        _store_slabs(o_ref, (), y)


def _experts(cfg, xs, blk_e, n_used, wg, wu, wd):
    n_slots, sr, sw = xs.shape
    d, ff, bm = cfg.d, cfg.ff, cfg.moe_block
    nb = n_slots // bm

    def blk(b, be, nu):
        return jnp.minimum(b, nu[0] - 1)

    grid_spec = pltpu.PrefetchScalarGridSpec(
        num_scalar_prefetch=2,
        grid=(nb,),
        in_specs=[pl.BlockSpec((bm * sr, sw), lambda b, be, nu: (blk(b, be, nu), 0)),
                  pl.BlockSpec((1, d, ff), lambda b, be, nu: (be[blk(b, be, nu)], 0, 0)),
                  pl.BlockSpec((1, d, ff), lambda b, be, nu: (be[blk(b, be, nu)], 0, 0)),
                  pl.BlockSpec((1, ff, d), lambda b, be, nu: (be[blk(b, be, nu)], 0, 0))],
        out_specs=pl.BlockSpec((bm * sr, sw), lambda b, be, nu: (blk(b, be, nu), 0)),
        scratch_shapes=[pltpu.VMEM((bm, d), BF)],
    )
    ys = pl.pallas_call(
        _ex_kernel,
        grid_spec=grid_spec,
        out_shape=jax.ShapeDtypeStruct((n_slots * sr, sw), F32),
        compiler_params=_params(("arbitrary",)),
        name="moe_experts",
    )(blk_e, n_used, xs.reshape(n_slots * sr, sw), wg, wu, wd)
    return ys.reshape(n_slots, sr, sw)


def _cb_kernel(x_ref, y_ref, rg_ref, gt_ref, o_ref):
    tm, d = x_ref.shape
    rows, pitch = _slab(d)
    g0, g1 = rg_ref[:, 0:1], rg_ref[:, 1:2]
    for s in range(rows):
        sl = slice(s * LANES, (s + 1) * LANES)
        f = (_load_slab_piece(y_ref, (0,), tm, pitch, s) * g0
             + _load_slab_piece(y_ref, (1,), tm, pitch, s) * g1)
        o_ref[:, sl] = x_ref[:, sl] + gt_ref[0][:, sl] * f


def _combine(cfg, x, y2, rg, mods, layer):
    t, d = x.shape
    tm = 256
    _, pitch = _slab(d)
    return pl.pallas_call(
        _cb_kernel,
        grid=(t // tm,),
        in_specs=[pl.BlockSpec((tm, d), lambda i: (i, 0)),
                  pl.BlockSpec((2, tm * pitch, LANES), lambda i: (0, i, 0)),
                  pl.BlockSpec((tm, ROUTE_LANES), lambda i: (i, 0)),
                  _mod_spec(cfg, layer, 5, tm)],
        out_specs=pl.BlockSpec((tm, d), lambda i: (i, 0)),
        out_shape=jax.ShapeDtypeStruct((t, d), F32),
        input_output_aliases={0: 0},
        compiler_params=_params(("arbitrary",)),
        name="moe_combine",
    )(x, y2.reshape(2, t * pitch, LANES), rg, mods)


def _moe(cfg, x, g, mods, layer, w_grp, b_grp, w_exp, b_exp, wg, wu, wd):
    t, d = x.shape
    bm = cfg.moe_block
    pad = ROUTE_LANES - N_EXPERT_GROUPS - N_EXPERTS
    w_route = jnp.pad(jnp.concatenate([w_grp, w_exp], axis=1), ((0, 0), (0, pad)))
    b_route = jnp.pad(jnp.concatenate([b_grp, b_exp]), (0, pad)).reshape(1, ROUTE_LANES)
    hs, ri, rg, cnt = _norm_route(cfg, x, g, mods, layer, w_route, b_route)

    counts = cnt[0, :N_EXPERTS].astype(I32)
    padded = (counts + bm - 1) // bm * bm
    ends = jnp.cumsum(padded)
    starts = ends - padded
    eid, rank = ri[:, 0:2], ri[:, 2:4]
    start_of = jnp.sum(jnp.where(eid[:, :, None] == jnp.arange(N_EXPERTS, dtype=I32), starts, 0), axis=-1)
    dest = (start_of + rank).reshape(-1)
    n_blocks = -(-2 * t // bm) + N_EXPERTS
    blk_e = jnp.minimum(jnp.sum(ends[None, :] <= (jnp.arange(n_blocks, dtype=I32) * bm)[:, None], axis=1),
                        N_EXPERTS - 1).astype(I32)
    n_used = (ends[-1:] // bm).astype(I32)

    xs = _dispatch(hs.reshape(t, _slab(d)[1], LANES), dest, n_blocks * bm)
    ys = _experts(cfg, xs, blk_e, n_used, wg, wu, wd)
    y2 = _gather_back(ys, dest, t)
    return _combine(cfg, x, y2, rg, mods, layer)


def _forward(cfg, x_prompt, x_sample, cache_k, cache_v, c, c_ctx, w_ada, b_ada, norm_mix_g, norm_ffn_g,
             fourier_w_in, fourier_w_out, na_w_qkv, na_w_out, na_rpb,
             router_grp_w, router_grp_b, router_exp_w, router_exp_b,
             expert_w_gate, expert_w_up, expert_w_down, final_norm_g):
    d, t = cfg.d, cfg.n_prompt + cfg.n_latent
    depth = w_ada.shape[0]
    x = jnp.concatenate([x_prompt.reshape(cfg.n_prompt, d), x_sample.reshape(cfg.n_latent, d)], axis=0)

    nb_lat = cfg.n_latent // cfg.latent_len
    cond = jnp.concatenate([c_ctx[None], c, jnp.zeros((COND_ROWS - 1 - nb_lat, d), F32)], axis=0)
    mods = _ada(cond, w_ada, b_ada).reshape(depth * COND_ROWS * N_MODS, 1, d)

    new_k, new_v = [], []
    for i in range(depth):
        j = i // 2
        h = _nm(cfg, x, norm_mix_g[i], mods, i, 1, 0)
        if i % 2 == 0:
            f = _fourier_mix(cfg, h, fourier_w_in[j].astype(BF))
            x = _mm_residual(cfg, f, fourier_w_out[j].astype(BF), x, mods, i, 2)
        else:
            w_qkv = na_w_qkv[j].astype(BF)
            scale = cfg.head_dim ** -0.5
            q_p = _mm(h, w_qkv, row0=0, rows=cfg.n_prompt, col0=0, cols=d, out_dtype=BF,
                      scaled_cols=d, scale=scale)
            k_p = _mm(h, w_qkv, row0=0, rows=cfg.n_prompt, col0=d, cols=d, out_dtype=F32)
            v_p = _mm(h, w_qkv, row0=0, rows=cfg.n_prompt, col0=2 * d, cols=d, out_dtype=F32)
            qkv_l = _mm(h, w_qkv, row0=cfg.n_prompt, rows=cfg.n_latent, col0=0, cols=3 * d, out_dtype=BF,
                        scaled_cols=d, scale=scale)
            new_k.append(k_p)
            new_v.append(v_p)
            o = _prompt_attention(cfg, q_p, k_p, v_p, t)
            t_bias = _expand_rpb(na_rpb[j])
            ck = cache_k[:, j].reshape(nb_lat, cfg.past, d)
            cv = cache_v[:, j].reshape(nb_lat, cfg.past, d)
            o = _latent_attention(cfg, qkv_l, ck, cv, t_bias, o)
            x = _mm_residual(cfg, o, na_w_out[j].astype(BF), x, mods, i, 2)
        x = _moe(cfg, x, norm_ffn_g[i], mods, i, router_grp_w[i], router_grp_b[i], router_exp_w[i],
                 router_exp_b[i], expert_w_gate[i].astype(BF), expert_w_up[i].astype(BF),
                 expert_w_down[i].astype(BF))

    y_prompt = _final_norm(x, final_norm_g, 0, cfg.n_prompt).reshape(x_prompt.shape)
    y_sample = _final_norm(x, final_norm_g, cfg.n_prompt, cfg.n_latent).reshape(x_sample.shape)
    nbp = x_prompt.shape[0]
    kv_shape = (nbp, cfg.prompt_len, cfg.heads, cfg.head_dim)
    new_k = jnp.stack([a.reshape(kv_shape) for a in new_k], axis=1)
    new_v = jnp.stack([a.reshape(kv_shape) for a in new_v], axis=1)
    return (y_prompt, y_sample, new_k, new_v)


def kernel(x_prompt, x_sample, cache_k, cache_v, c, c_ctx, w_ada, b_ada, norm_mix_g, norm_ffn_g,
           fourier_w_in, fourier_w_out, na_w_qkv, na_w_out, na_rpb,
           router_grp_w, router_grp_b, router_exp_w, router_exp_b,
           expert_w_gate, expert_w_up, expert_w_down, final_norm_g):
    bp, lp, d = x_prompt.shape
    bl, ll, _ = x_sample.shape
    heads = na_rpb.shape[1]
    cfg = Cfg(d=d, n_prompt=bp * lp, prompt_len=lp, n_latent=bl * ll, latent_len=ll, heads=heads,
              head_dim=d // heads, past=cache_k.shape[2], ff=expert_w_gate.shape[-1], moe_block=128)
    return _forward(cfg, x_prompt, x_sample, cache_k, cache_v, c, c_ctx, w_ada, b_ada, norm_mix_g,
                    norm_ffn_g, fourier_w_in, fourier_w_out, na_w_qkv, na_w_out, na_rpb,
                    router_grp_w, router_grp_b, router_exp_w, router_exp_b,
                    expert_w_gate, expert_w_up, expert_w_down, final_norm_g)
```

```python
import collections
import functools

import jax
import jax.numpy as jnp
from jax import lax
from jax.experimental import pallas as pl
from jax.experimental.pallas import tpu as pltpu

BF = jnp.bfloat16
F32 = jnp.float32
I32 = jnp.int32

NORM_EPS = 1e-6
MASKED = -1e30
GRID_W = 64
NA_ROWS = 8
NA_COLS = 16
FOURIER_GROUPS = 8
EXPERTS_PER_GROUP = 8
N_EXPERT_GROUPS = 4
N_EXPERTS = N_EXPERT_GROUPS * EXPERTS_PER_GROUP
N_MODS = 6
COND_ROWS = 8
LANES = 128
SUBLANES = 8
SLAB_PAD = 8
ROUTE_LANES = LANES
V7X_VMEM_LIMIT = 56 * 2**20

Cfg = collections.namedtuple(
    "Cfg", "d n_prompt prompt_len n_latent latent_len heads head_dim past ff moe_block")


def _params(sem, vmem=V7X_VMEM_LIMIT):
    return pltpu.CompilerParams(dimension_semantics=sem, vmem_limit_bytes=vmem)


def _silu(x):
    return x / (1.0 + jnp.exp(-x))


def _group_of_tile(i, tm, cfg):
    start = i * tm
    return jnp.where(start < cfg.n_prompt, 0, 1 + (start - cfg.n_prompt) // cfg.latent_len)


def _mod_spec(cfg, layer, which, tm, width=None, col=False):
    width = cfg.d if width is None else width
    base = layer * COND_ROWS * N_MODS + which
    if col:
        return pl.BlockSpec((1, 1, width), lambda i, j: (base + _group_of_tile(i, tm, cfg) * N_MODS, 0, j))
    return pl.BlockSpec((1, 1, width), lambda i: (base + _group_of_tile(i, tm, cfg) * N_MODS, 0, 0))


def _ada_kernel(c_ref, w_ref, b_ref, o_ref):
    s = _silu(c_ref[...]).astype(BF)
    o_ref[0] = jnp.dot(s, w_ref[0].astype(BF), preferred_element_type=F32) + b_ref[0]


def _ada(cond, w_ada, b_ada):
    depth, d, n = w_ada.shape
    tn = min(512, n)
    return pl.pallas_call(
        _ada_kernel,
        grid=(depth, n // tn),
        in_specs=[pl.BlockSpec((COND_ROWS, d), lambda l, j: (0, 0)),
                  pl.BlockSpec((1, d, tn), lambda l, j: (l, 0, j)),
                  pl.BlockSpec((1, 1, tn), lambda l, j: (l, 0, j))],
        out_specs=pl.BlockSpec((1, COND_ROWS, tn), lambda l, j: (l, 0, j)),
        out_shape=jax.ShapeDtypeStruct((depth, COND_ROWS, n), F32),
        compiler_params=_params(("arbitrary", "arbitrary")),
        name="ada",
    )(cond, w_ada, b_ada.reshape(depth, 1, n))


def _norm_mod(x, g, sc, sh):
    y = x * lax.rsqrt(jnp.mean(x * x, axis=-1, keepdims=True) + NORM_EPS) * g
    return y * (1.0 + sc) + sh


def _nm_kernel(x_ref, g_ref, sc_ref, sh_ref, o_ref):
    o_ref[...] = _norm_mod(x_ref[...], g_ref[...], sc_ref[0], sh_ref[0]).astype(o_ref.dtype)


def _nm(cfg, x, g, mods, layer, sc_k, sh_k):
    t, d = x.shape
    tm = 512
    return pl.pallas_call(
        _nm_kernel,
        grid=(t // tm,),
        in_specs=[pl.BlockSpec((tm, d), lambda i: (i, 0)),
                  pl.BlockSpec((1, d), lambda i: (0, 0)),
                  _mod_spec(cfg, layer, sc_k, tm),
                  _mod_spec(cfg, layer, sh_k, tm)],
        out_specs=pl.BlockSpec((tm, d), lambda i: (i, 0)),
        out_shape=jax.ShapeDtypeStruct((t, d), BF),
        compiler_params=_params(("arbitrary",)),
        name="norm_mod",
    )(x, g.reshape(1, d), mods, mods)


def _fn_kernel(x_ref, g_ref, o_ref):
    x = x_ref[...]
    o_ref[...] = x * lax.rsqrt(jnp.mean(x * x, axis=-1, keepdims=True) + NORM_EPS) * g_ref[...]


def _final_norm(x, g, row0, rows):
    d = x.shape[1]
    tm = 512
    r0 = row0 // tm
    return pl.pallas_call(
        _fn_kernel,
        grid=(rows // tm,),
        in_specs=[pl.BlockSpec((tm, d), lambda i: (r0 + i, 0)),
                  pl.BlockSpec((1, d), lambda i: (0, 0))],
        out_specs=pl.BlockSpec((tm, d), lambda i: (i, 0)),
        out_shape=jax.ShapeDtypeStruct((rows, d), F32),
        compiler_params=_params(("arbitrary",)),
        name="final_norm",
    )(x, g.reshape(1, d))


def _mm_kernel(x_ref, w_ref, o_ref, *, n_scaled, scale):
    acc = jnp.dot(x_ref[...], w_ref[...], preferred_element_type=F32)
    if n_scaled:
        acc = acc * jnp.where(pl.program_id(1) < n_scaled, scale, 1.0)
    o_ref[...] = acc.astype(o_ref.dtype)


def _mm(x, w, *, row0, rows, col0, cols, out_dtype, scaled_cols=0, scale=1.0):
    k = x.shape[1]
    bm, bn = min(1024, rows), min(512, cols)
    r0, c0 = row0 // bm, col0 // bn
    return pl.pallas_call(
        functools.partial(_mm_kernel, n_scaled=scaled_cols // bn, scale=scale),
        grid=(rows // bm, cols // bn),
        in_specs=[pl.BlockSpec((bm, k), lambda i, j: (r0 + i, 0)),
                  pl.BlockSpec((k, bn), lambda i, j: (0, c0 + j))],
        out_specs=pl.BlockSpec((bm, bn), lambda i, j: (i, j)),
        out_shape=jax.ShapeDtypeStruct((rows, cols), out_dtype),
        compiler_params=_params(("arbitrary", "arbitrary")),
        name="matmul",
    )(x, w)


def _mr_kernel(x_ref, w_ref, r_ref, g_ref, o_ref):
    acc = jnp.dot(x_ref[...], w_ref[...], preferred_element_type=F32)
    o_ref[...] = r_ref[...] + g_ref[0] * acc


def _mm_residual(cfg, x, w, res, mods, layer, gate_k):
    t, k = x.shape
    n = w.shape[1]
    bm, bn = 1024, min(512, n)
    return pl.pallas_call(
        _mr_kernel,
        grid=(t // bm, n // bn),
        in_specs=[pl.BlockSpec((bm, k), lambda i, j: (i, 0)),
                  pl.BlockSpec((k, bn), lambda i, j: (0, j)),
                  pl.BlockSpec((bm, bn), lambda i, j: (i, j)),
                  _mod_spec(cfg, layer, gate_k, bm, width=bn, col=True)],
        out_specs=pl.BlockSpec((bm, bn), lambda i, j: (i, j)),
        out_shape=jax.ShapeDtypeStruct((t, n), F32),
        input_output_aliases={2: 0},
        compiler_params=_params(("arbitrary", "arbitrary")),
        name="matmul_residual",
    )(x, w, res, mods)


def _fa_kernel(x_ref, w_ref, cs_ref, o_ref):
    u = jnp.dot(x_ref[...], w_ref[...], preferred_element_type=F32).astype(BF)
    v = jnp.dot(u, cs_ref[...], preferred_element_type=F32)
    n = u.shape[1]
    o_ref[0] = v[:, :n].astype(BF)
    o_ref[1] = v[:, n:].astype(BF)


def _fourier_in(h, w_in, cs):
    t, d = h.shape
    gd = d // FOURIER_GROUPS
    bm = 512
    return pl.pallas_call(
        _fa_kernel,
        grid=(t // bm, FOURIER_GROUPS),
        in_specs=[pl.BlockSpec((bm, d), lambda i, j: (i, 0)),
                  pl.BlockSpec((d, gd), lambda i, j: (0, j)),
                  pl.BlockSpec((gd, 2 * gd), lambda i, j: (0, 0))],
        out_specs=pl.BlockSpec((2, bm, gd), lambda i, j: (0, i, j)),
        out_shape=jax.ShapeDtypeStruct((2, t, d), BF),
        compiler_params=_params(("arbitrary", "arbitrary")),
        name="fourier_in",
    )(h, w_in, cs)


def _fb_kernel(ac_ref, as_ref, vc_ref, vs_ref, prev_ref, o_ref):
    del prev_ref
    acc = jnp.dot(ac_ref[0], vc_ref[0], preferred_element_type=F32)
    acc = acc + jnp.dot(as_ref[0], vs_ref[0], preferred_element_type=F32)
    o_ref[...] = acc.astype(o_ref.dtype)


def _fourier_seq(a, v, prev, *, row0, n_seq, seq_len, bm, bn):
    _, t, d = v.shape
    rb0, ob0 = row0 // seq_len, row0 // bm
    ni = seq_len // bm
    return pl.pallas_call(
        _fb_kernel,
        grid=(n_seq, d // bn, ni),
        in_specs=[pl.BlockSpec((1, bm, seq_len), lambda b, j, i: (0, i, 0)),
                  pl.BlockSpec((1, bm, seq_len), lambda b, j, i: (1, i, 0)),
                  pl.BlockSpec((1, seq_len, bn), lambda b, j, i: (0, rb0 + b, j)),
                  pl.BlockSpec((1, seq_len, bn), lambda b, j, i: (1, rb0 + b, j)),
                  pl.BlockSpec(memory_space=pl.ANY)],
        out_specs=pl.BlockSpec((bm, bn), lambda b, j, i: (ob0 + b * ni + i, j)),
        out_shape=jax.ShapeDtypeStruct((t, d), BF),
        input_output_aliases={4: 0},
        compiler_params=_params(("arbitrary", "arbitrary", "arbitrary")),
        name="fourier_seq",
    )(a, a, v, v, prev)


def _dft_tables(n):
    j = lax.broadcasted_iota(I32, (n, n), 0)
    k = lax.broadcasted_iota(I32, (n, n), 1)
    ang = ((j * k) % n).astype(F32) * (2.0 * jnp.pi / n)
    return jnp.cos(ang), jnp.sin(ang)


def _dft_tables_big(n, r=64):
    a = lax.broadcasted_iota(I32, (r, n), 0)
    k = lax.broadcasted_iota(I32, (r, n), 1)
    ang_hi = ((a * k) % r).astype(F32) * (2.0 * jnp.pi / r)
    ang_lo = ((a * k) % n).astype(F32) * (2.0 * jnp.pi / n)
    ch, sh = jnp.cos(ang_hi)[:, None, :], jnp.sin(ang_hi)[:, None, :]
    cl, sl = jnp.cos(ang_lo)[None, :, :], jnp.sin(ang_lo)[None, :, :]
    cos = (ch * cl - sh * sl).reshape(n, n)
    sin = (sh * cl + ch * sl).reshape(n, n)
    return cos, sin


def _fourier_mix(cfg, h, w_in):
    d = cfg.d
    gd = d // FOURIER_GROUPS
    cc, sc = _dft_tables(gd)
    cs = (jnp.concatenate([cc, sc], axis=1) * gd ** -0.5).astype(BF)
    v = _fourier_in(h, w_in, cs)
    cp, sp = _dft_tables(cfg.prompt_len)
    a_p = (jnp.stack([cp, -sp]) * cfg.prompt_len ** -0.5).astype(BF)
    cl, sl = _dft_tables_big(cfg.latent_len)
    a_l = (jnp.stack([cl, -sl]) * cfg.latent_len ** -0.5).astype(BF)
    f = _fourier_seq(a_p, v, h, row0=0, n_seq=cfg.n_prompt // cfg.prompt_len,
                     seq_len=cfg.prompt_len, bm=cfg.prompt_len, bn=d)
    f = _fourier_seq(a_l, v, f, row0=cfg.n_prompt, n_seq=cfg.n_latent // cfg.latent_len,
                     seq_len=cfg.latent_len, bm=512, bn=512)
    return f


def _softmax_pv(scores, values):
    m = functools.reduce(jnp.maximum, [jnp.max(s, axis=-1, keepdims=True) for s in scores])
    ps = [jnp.exp(s - m) for s in scores]
    l = functools.reduce(jnp.add, [jnp.sum(p, axis=-1, keepdims=True) for p in ps])
    o = functools.reduce(jnp.add, [jnp.dot(p.astype(BF), v, preferred_element_type=F32)
                                   for p, v in zip(ps, values)])
    return o / l


def _qk(q, k):
    return lax.dot_general(q, k, (((1,), (1,)), ((), ())), preferred_element_type=F32)


def _pa_kernel(q_ref, k_ref, v_ref, *rest, heads, dh):
    o_ref = rest[-1]
    for h in range(heads):
        sl = slice(h * dh, (h + 1) * dh)
        s = _qk(q_ref[:, sl], k_ref[:, sl].astype(BF))
        o_ref[:, sl] = _softmax_pv([s], [v_ref[:, sl].astype(BF)]).astype(o_ref.dtype)


def _prompt_attention(cfg, q, k, v, prev):
    lp, d = cfg.prompt_len, cfg.d
    hb = min(8, cfg.heads)
    w = hb * cfg.head_dim
    spec = pl.BlockSpec((lp, w), lambda b, g: (b, g))
    return pl.pallas_call(
        functools.partial(_pa_kernel, heads=hb, dh=cfg.head_dim),
        grid=(cfg.n_prompt // lp, d // w),
        in_specs=[spec, spec, spec, pl.BlockSpec(memory_space=pl.ANY)],
        out_specs=spec,
        out_shape=jax.ShapeDtypeStruct(prev.shape, BF),
        input_output_aliases={3: 0},
        compiler_params=_params(("arbitrary", "arbitrary")),
        name="prompt_attention",
    )(q, k, v, prev)


def _rx_kernel(r_ref, o_ref):
    r = r_ref[...]
    hi = r.astype(BF)
    r1 = r - hi.astype(F32)
    mid = r1.astype(BF)
    lo = (r1 - mid.astype(F32)).astype(BF)
    n = o_ref.shape[1]
    idx = lax.broadcasted_iota(I32, (ROUTE_LANES, n), 1)
    row = lax.broadcasted_iota(I32, (ROUTE_LANES, n), 0)
    qc, kc = idx >> 7, idx & (GRID_W - 1)
    dc = jnp.clip(kc - qc + NA_COLS - 1, 0, 2 * NA_COLS - 2)
    onehot = jnp.where(row == dc, 1.0, 0.0).astype(BF)
    t = (jnp.dot(hi, onehot, preferred_element_type=F32)
         + jnp.dot(mid, onehot, preferred_element_type=F32)
         + jnp.dot(lo, onehot, preferred_element_type=F32))
    col = lax.broadcasted_iota(I32, (1, n), 1)
    qc1, kc1 = col >> 7, col & (GRID_W - 1)
    c0 = jnp.clip(qc1 - NA_COLS // 2, 0, GRID_W - NA_COLS)
    o_ref[...] = jnp.where((kc1 >= c0) & (kc1 < c0 + NA_COLS), t, MASKED)


def _expand_rpb(rpb):
    heads, nr, nc = rpb.shape
    rows = heads * nr
    tr = rows // 4
    r2 = jnp.pad(rpb.reshape(rows, nc), ((0, 0), (0, ROUTE_LANES - nc)))
    out = pl.pallas_call(
        _rx_kernel,
        grid=(rows // tr,),
        in_specs=[pl.BlockSpec((tr, ROUTE_LANES), lambda i: (i, 0))],
        out_specs=pl.BlockSpec((tr, GRID_W * 2 * GRID_W), lambda i: (i, 0)),
        out_shape=jax.ShapeDtypeStruct((rows, GRID_W * 2 * GRID_W), F32),
        compiler_params=_params(("arbitrary",)),
        name="rpb_expand",
    )(r2)
    return out.reshape(heads, nr, GRID_W, 2 * GRID_W)


NA_QROWS = 4
NA_KROWS = NA_QROWS + NA_ROWS
NA_VARIANTS = ((0, 0), (2 * NA_QROWS, NA_QROWS), (GRID_W - NA_QROWS, GRID_W - NA_KROWS))


def _na_kernel(q_ref, k_ref, v_ref, kc_ref, vc_ref, t_ref, *rest):
    o_ref, bias_ref = rest[-2], rest[-1]
    w = GRID_W

    @pl.when(pl.program_id(1) == 0)
    def _build_bias():
        left = lax.broadcasted_iota(I32, (w, 2 * w), 1) < w
        masked = jnp.full((w, 2 * w), MASKED, F32)

        def table(r, kr):
            rs = min(max(r - NA_ROWS // 2, 0), w - NA_ROWS)
            return t_ref[0, kr - r + NA_ROWS - 1] if rs <= kr < rs + NA_ROWS else masked

        for var, (r0, k0) in enumerate(NA_VARIANTS):
            for i in range(NA_QROWS):
                for j in range(0, NA_KROWS, 2):
                    pair = jnp.where(left, table(r0 + i, k0 + j), table(r0 + i, k0 + j + 1))
                    bias_ref[var, i * w:(i + 1) * w, j * w:(j + 2) * w] = pair

    kc = kc_ref[0].astype(BF)
    vc = vc_ref[0].astype(BF)
    nq = NA_QROWS * w
    nk = NA_KROWS * w
    n_blocks = w // NA_QROWS

    def block(qb, carry):
        var = jnp.where(qb == 0, 0, jnp.where(qb == n_blocks - 1, 2, 1))
        k0 = jnp.clip(qb * NA_QROWS - NA_ROWS // 2, 0, w - NA_KROWS)
        q0 = pl.multiple_of(qb * nq, nq)
        ks = pl.multiple_of(k0 * w, w)
        q = q_ref[pl.ds(q0, nq), :]
        s_lat = _qk(q, k_ref[pl.ds(ks, nk), :]) + bias_ref[var]
        s_ctx = _qk(q, kc)
        o = _softmax_pv([s_lat, s_ctx], [v_ref[pl.ds(ks, nk), :], vc])
        o_ref[pl.ds(q0, nq), :] = o.astype(o_ref.dtype)
        return carry

    lax.fori_loop(0, n_blocks, block, 0)


def _latent_attention(cfg, qkv, cache_k, cache_v, t_bias, prev):
    d, dh, heads, ls = cfg.d, cfg.head_dim, cfg.heads, cfg.latent_len
    nb = cfg.n_latent // ls
    rb0 = cfg.n_prompt // ls
    w = GRID_W
    return pl.pallas_call(
        _na_kernel,
        grid=(heads, nb),
        in_specs=[pl.BlockSpec((ls, dh), lambda h, b: (b, h)),
                  pl.BlockSpec((ls, dh), lambda h, b: (b, heads + h)),
                  pl.BlockSpec((ls, dh), lambda h, b: (b, 2 * heads + h)),
                  pl.BlockSpec((1, cfg.past, dh), lambda h, b: (b, 0, h)),
                  pl.BlockSpec((1, cfg.past, dh), lambda h, b: (b, 0, h)),
                  pl.BlockSpec((1, 2 * NA_ROWS - 1, w, 2 * w), lambda h, b: (h, 0, 0, 0)),
                  pl.BlockSpec(memory_space=pl.ANY)],
        out_specs=pl.BlockSpec((ls, dh), lambda h, b: (rb0 + b, h)),
        out_shape=jax.ShapeDtypeStruct(prev.shape, BF),
        scratch_shapes=[pltpu.VMEM((len(NA_VARIANTS), NA_QROWS * w, NA_KROWS * w), F32)],
        input_output_aliases={6: 0},
        compiler_params=_params(("arbitrary", "arbitrary")),
        name="latent_attention",
    )(qkv, qkv, qkv, cache_k, cache_v, t_bias, prev)


def _slab(d):
    rows = d // LANES
    return rows, rows + SLAB_PAD


def _store_slabs(ref, x):
    n, d = x.shape
    rows, pitch = _slab(d)
    for s in range(rows):
        ref[pl.ds(s, n, stride=pitch), :] = x[:, s * LANES:(s + 1) * LANES]
    for s in range(rows, pitch):
        ref[pl.ds(s, n, stride=pitch), :] = jnp.zeros((n, LANES), ref.dtype)


def _load_slab_piece(ref, lead, n, pitch, s):
    return ref[lead + (pl.ds(s, n, stride=pitch), slice(None))]


def _nmr_kernel(x_ref, g_ref, sc_ref, sh_ref, wr_ref, br_ref, hs_ref, ri_ref, rg_ref, cnt_ref, base_ref):
    tm, d = x_ref.shape

    @pl.when(pl.program_id(0) == 0)
    def _init():
        base_ref[...] = jnp.zeros_like(base_ref)

    h = _norm_mod(x_ref[...], g_ref[...], sc_ref[0], sh_ref[0])
    _store_slabs(hs_ref, h)

    h_hi = h.astype(BF)
    h_lo = (h - h_hi.astype(F32)).astype(BF)
    wr = wr_ref[...]
    w_hi = wr.astype(BF)
    w_lo = (wr - w_hi.astype(F32)).astype(BF)
    logits = (jnp.dot(h_hi, w_hi, preferred_element_type=F32)
              + jnp.dot(h_lo, w_hi, preferred_element_type=F32)
              + jnp.dot(h_hi, w_lo, preferred_element_type=F32)) + br_ref[...]

    lane = lax.broadcasted_iota(I32, (tm, ROUTE_LANES), 1)
    lane_f = lane.astype(F32)

    def first_argmax(v):
        m = jnp.max(v, axis=-1, keepdims=True)
        first = jnp.min(jnp.where(v == m, lane_f, float(ROUTE_LANES)), axis=-1, keepdims=True)
        return m, first.astype(I32)

    gl = jnp.where(lane < N_EXPERT_GROUPS, logits, MASKED)
    gmax, gi = first_argmax(gl)
    g_w = 1.0 / jnp.sum(jnp.exp(gl - gmax), axis=-1, keepdims=True)
    e0 = N_EXPERT_GROUPS + gi * EXPERTS_PER_GROUP
    el = jnp.where((lane >= e0) & (lane < e0 + EXPERTS_PER_GROUP), logits, MASKED)
    m1, i1 = first_argmax(el)
    m2, i2 = first_argmax(jnp.where(lane == i1, MASKED, el))
    tt = jnp.exp(m2 - m1)
    w1 = 1.0 / (1.0 + tt)
    eid0, eid1 = i1 - N_EXPERT_GROUPS, i2 - N_EXPERT_GROUPS

    oh = jnp.where((lane == eid0) | (lane == eid1), 1.0, 0.0)
    row = lax.broadcasted_iota(I32, (tm, tm), 0)
    colt = lax.broadcasted_iota(I32, (tm, tm), 1)
    earlier = jnp.where(row > colt, 1.0, 0.0).astype(BF)
    before = jnp.dot(earlier, oh.astype(BF), preferred_element_type=F32) + base_ref[0:1, :]
    rank0 = jnp.sum(jnp.where(lane == eid0, before, 0.0), axis=-1, keepdims=True).astype(I32)
    rank1 = jnp.sum(jnp.where(lane == eid1, before, 0.0), axis=-1, keepdims=True).astype(I32)
    base_ref[...] = base_ref[...] + jnp.sum(oh, axis=0, keepdims=True)

    ri_ref[...] = jnp.where(lane == 0, eid0, jnp.where(lane == 1, eid1,
                            jnp.where(lane == 2, rank0, jnp.where(lane == 3, rank1, 0))))
    rg_ref[...] = jnp.where(lane == 0, g_w * w1, jnp.where(lane == 1, g_w * (tt * w1), 0.0))
    cnt_ref[...] = base_ref[...]


def _norm_route(cfg, x, g, mods, layer, w_route, b_route):
    t, d = x.shape
    tm = 256
    _, pitch = _slab(d)
    return pl.pallas_call(
        _nmr_kernel,
        grid=(t // tm,),
        in_specs=[pl.BlockSpec((tm, d), lambda i: (i, 0)),
                  pl.BlockSpec((1, d), lambda i: (0, 0)),
                  _mod_spec(cfg, layer, 4, tm),
                  _mod_spec(cfg, layer, 3, tm),
                  pl.BlockSpec((d, ROUTE_LANES), lambda i: (0, 0)),
                  pl.BlockSpec((1, ROUTE_LANES), lambda i: (0, 0))],
        out_specs=[pl.BlockSpec((tm * pitch, LANES), lambda i: (i, 0)),
                   pl.BlockSpec((tm, ROUTE_LANES), lambda i: (i, 0)),
                   pl.BlockSpec((tm, ROUTE_LANES), lambda i: (i, 0)),
                   pl.BlockSpec((SUBLANES, ROUTE_LANES), lambda i: (0, 0))],
        out_shape=[jax.ShapeDtypeStruct((t * pitch, LANES), F32),
                   jax.ShapeDtypeStruct((t, ROUTE_LANES), I32),
                   jax.ShapeDtypeStruct((t, ROUTE_LANES), F32),
                   jax.ShapeDtypeStruct((SUBLANES, ROUTE_LANES), F32)],
        scratch_shapes=[pltpu.VMEM((SUBLANES, ROUTE_LANES), F32)],
        compiler_params=_params(("arbitrary",)),
        name="norm_route",
    )(x, g.reshape(1, d), mods, mods, w_route, b_route)


MOVE_TOKENS = 256


def _wait_bytes_of(buf_ref, sem):
    pltpu.make_async_copy(buf_ref, buf_ref, sem).wait()


def _ds_kernel(dest_ref, hs_ref, init_ref, xs_ref, sem):
    del init_ref
    n = MOVE_TOKENS
    pitch = hs_ref.shape[0] // n
    rows = pitch - SLAB_PAD

    def issue(tk, carry):
        src = hs_ref.at[pl.ds(pl.multiple_of(tk * pitch, SUBLANES), rows), :]
        for k in range(2):
            pltpu.make_async_copy(src, xs_ref.at[dest_ref[0, 0, 2 * tk + k], pl.ds(0, rows), :], sem).start()
        return carry

    lax.fori_loop(0, n, issue, 0)
    _wait_bytes_of(xs_ref.at[pl.ds(0, 2 * n), pl.ds(0, rows), :], sem)


def _dispatch(hs, dest, n_slots):
    n = MOVE_TOKENS
    t = dest.shape[0] // 2
    pitch = hs.shape[0] // t
    return pl.pallas_call(
        _ds_kernel,
        grid=(t // n,),
        in_specs=[pl.BlockSpec((1, 1, 2 * n), lambda i: (i, 0, 0), memory_space=pltpu.SMEM),
                  pl.BlockSpec((n * pitch, LANES), lambda i: (i, 0)),
                  pl.BlockSpec(memory_space=pl.ANY)],
        out_specs=pl.BlockSpec(memory_space=pl.ANY),
        out_shape=jax.ShapeDtypeStruct((n_slots, pitch, LANES), F32),
        scratch_shapes=[pltpu.SemaphoreType.DMA],
        input_output_aliases={2: 0},
        compiler_params=_params(("arbitrary",)),
        name="moe_dispatch",
    )(dest.reshape(t // n, 1, 2 * n), hs, jnp.zeros((n_slots, pitch, LANES), F32))


def _ex_kernel(blk_e_ref, n_used_ref, x_ref, wg_ref, wu_ref, wd_ref, o_ref, xb_ref):
    del blk_e_ref
    bm, d = xb_ref.shape
    rows, pitch = _slab(d)

    @pl.when(pl.program_id(0) < n_used_ref[0])
    def _compute():
        for s in range(rows):
            xb_ref[:, s * LANES:(s + 1) * LANES] = _load_slab_piece(x_ref, (), bm, pitch, s).astype(BF)
        x = xb_ref[...]
        g = jnp.dot(x, wg_ref[0], preferred_element_type=F32)
        u = jnp.dot(x, wu_ref[0], preferred_element_type=F32)
        y = jnp.dot((_silu(g) * u).astype(BF), wd_ref[0], preferred_element_type=F32)
        _store_slabs(o_ref, y)


def _experts(cfg, xs, blk_e, n_used, wg, wu, wd):
    n_slots, sr, sw = xs.shape
    d, ff, bm = cfg.d, cfg.ff, cfg.moe_block
    nb = n_slots // bm

    def blk(b, be, nu):
        return jnp.minimum(b, nu[0] - 1)

    grid_spec = pltpu.PrefetchScalarGridSpec(
        num_scalar_prefetch=2,
        grid=(nb,),
        in_specs=[pl.BlockSpec((bm * sr, sw), lambda b, be, nu: (blk(b, be, nu), 0)),
                  pl.BlockSpec((1, d, ff), lambda b, be, nu: (be[blk(b, be, nu)], 0, 0)),
                  pl.BlockSpec((1, d, ff), lambda b, be, nu: (be[blk(b, be, nu)], 0, 0)),
                  pl.BlockSpec((1, ff, d), lambda b, be, nu: (be[blk(b, be, nu)], 0, 0))],
        out_specs=pl.BlockSpec((bm * sr, sw), lambda b, be, nu: (blk(b, be, nu), 0)),
        scratch_shapes=[pltpu.VMEM((bm, d), BF)],
    )
    ys = pl.pallas_call(
        _ex_kernel,
        grid_spec=grid_spec,
        out_shape=jax.ShapeDtypeStruct((n_slots * sr, sw), F32),
        input_output_aliases={2: 0},
        compiler_params=_params(("arbitrary",)),
        name="moe_experts",
    )(blk_e, n_used, xs.reshape(n_slots * sr, sw), wg, wu, wd)
    return ys.reshape(n_slots, sr, sw)


def _cb_kernel(dcur_ref, dnext_ref, x_ref, ys_ref, rg_ref, gt_ref, o_ref, ybuf_ref, sems):
    tm, d = x_ref.shape
    rows, pitch = _slab(d)
    i, n_steps = pl.program_id(0), pl.num_programs(0)

    def start_gather(dref, slot):
        def issue(tk, carry):
            for k in range(2):
                row0 = pl.multiple_of((k * tm + tk) * pitch, SUBLANES)
                pltpu.make_async_copy(ys_ref.at[dref[0, 0, 2 * tk + k], pl.ds(0, rows), :],
                                      ybuf_ref.at[slot, pl.ds(row0, rows), :], sems.at[slot]).start()
            return carry
        lax.fori_loop(0, tm, issue, 0)

    @pl.when(i == 0)
    def _first():
        start_gather(dcur_ref, 0)

    @pl.when(i + 1 < n_steps)
    def _next():
        start_gather(dnext_ref, (i + 1) % 2)

    slot = i % 2
    _wait_bytes_of(ybuf_ref.at[slot, pl.ds(0, 2 * tm * rows), :], sems.at[slot])
    g0, g1 = rg_ref[:, 0:1], rg_ref[:, 1:2]
    for s in range(rows):
        sl = slice(s * LANES, (s + 1) * LANES)
        f = (ybuf_ref[slot, pl.ds(s, tm, stride=pitch), :] * g0
             + ybuf_ref[slot, pl.ds(tm * pitch + s, tm, stride=pitch), :] * g1)
        o_ref[:, sl] = x_ref[:, sl] + gt_ref[0][:, sl] * f


def _combine(cfg, x, ys, dest, rg, mods, layer):
    t, d = x.shape
    tm = MOVE_TOKENS
    _, pitch = _slab(d)
    n = t // tm
    dest3 = dest.reshape(n, 1, 2 * tm)
    return pl.pallas_call(
        _cb_kernel,
        grid=(n,),
        in_specs=[pl.BlockSpec((1, 1, 2 * tm), lambda i: (i, 0, 0), memory_space=pltpu.SMEM),
                  pl.BlockSpec((1, 1, 2 * tm), lambda i: (jnp.minimum(i + 1, n - 1), 0, 0),
                               memory_space=pltpu.SMEM),
                  pl.BlockSpec((tm, d), lambda i: (i, 0)),
                  pl.BlockSpec(memory_space=pl.ANY),
                  pl.BlockSpec((tm, ROUTE_LANES), lambda i: (i, 0)),
                  _mod_spec(cfg, layer, 5, tm)],
        out_specs=pl.BlockSpec((tm, d), lambda i: (i, 0)),
        out_shape=jax.ShapeDtypeStruct((t, d), F32),
        scratch_shapes=[pltpu.VMEM((2, 2 * tm * pitch, LANES), F32),
                        pltpu.SemaphoreType.DMA((2,))],
        input_output_aliases={2: 0},
        compiler_params=_params(("arbitrary",)),
        name="moe_combine",
    )(dest3, dest3, x, ys, rg, mods)


def _moe(cfg, x, g, mods, layer, w_grp, b_grp, w_exp, b_exp, wg, wu, wd):
    t, d = x.shape
    bm = cfg.moe_block
    pad = ROUTE_LANES - N_EXPERT_GROUPS - N_EXPERTS
    w_route = jnp.pad(jnp.concatenate([w_grp, w_exp], axis=1), ((0, 0), (0, pad)))
    b_route = jnp.pad(jnp.concatenate([b_grp, b_exp]), (0, pad)).reshape(1, ROUTE_LANES)
    hs, ri, rg, cnt = _norm_route(cfg, x, g, mods, layer, w_route, b_route)

    counts = cnt[0, :N_EXPERTS].astype(I32)
    padded = (counts + bm - 1) // bm * bm
    ends = jnp.cumsum(padded)
    starts = ends - padded
    eid, rank = ri[:, 0:2], ri[:, 2:4]
    start_of = jnp.sum(jnp.where(eid[:, :, None] == jnp.arange(N_EXPERTS, dtype=I32), starts, 0), axis=-1)
    dest = (start_of + rank).reshape(-1)
    n_blocks = -(-2 * t // bm) + N_EXPERTS
    blk_e = jnp.minimum(jnp.sum(ends[None, :] <= (jnp.arange(n_blocks, dtype=I32) * bm)[:, None], axis=1),
                        N_EXPERTS - 1).astype(I32)
    n_used = (ends[-1:] // bm).astype(I32)

    xs = _dispatch(hs, dest, n_blocks * bm)
    ys = _experts(cfg, xs, blk_e, n_used, wg, wu, wd)
    return _combine(cfg, x, ys, dest, rg, mods, layer)


def _forward(cfg, x_prompt, x_sample, cache_k, cache_v, c, c_ctx, w_ada, b_ada, norm_mix_g, norm_ffn_g,
             fourier_w_in, fourier_w_out, na_w_qkv, na_w_out, na_rpb,
             router_grp_w, router_grp_b, router_exp_w, router_exp_b,
             expert_w_gate, expert_w_up, expert_w_down, final_norm_g):
    d, t = cfg.d, cfg.n_prompt + cfg.n_latent
    depth = w_ada.shape[0]
    x = jnp.concatenate([x_prompt.reshape(cfg.n_prompt, d), x_sample.reshape(cfg.n_latent, d)], axis=0)

    nb_lat = cfg.n_latent // cfg.latent_len
    cond = jnp.concatenate([c_ctx[None], c, jnp.zeros((COND_ROWS - 1 - nb_lat, d), F32)], axis=0)
    mods = _ada(cond, w_ada, b_ada).reshape(depth * COND_ROWS * N_MODS, 1, d)

    new_k, new_v = [], []
    for i in range(depth):
        j = i // 2
        h = _nm(cfg, x, norm_mix_g[i], mods, i, 1, 0)
        if i % 2 == 0:
            f = _fourier_mix(cfg, h, fourier_w_in[j].astype(BF))
            x = _mm_residual(cfg, f, fourier_w_out[j].astype(BF), x, mods, i, 2)
        else:
            w_qkv = na_w_qkv[j].astype(BF)
            scale = cfg.head_dim ** -0.5
            q_p = _mm(h, w_qkv, row0=0, rows=cfg.n_prompt, col0=0, cols=d, out_dtype=BF,
                      scaled_cols=d, scale=scale)
            k_p = _mm(h, w_qkv, row0=0, rows=cfg.n_prompt, col0=d, cols=d, out_dtype=F32)
            v_p = _mm(h, w_qkv, row0=0, rows=cfg.n_prompt, col0=2 * d, cols=d, out_dtype=F32)
            qkv_l = _mm(h, w_qkv, row0=cfg.n_prompt, rows=cfg.n_latent, col0=0, cols=3 * d, out_dtype=BF,
                        scaled_cols=d, scale=scale)
            new_k.append(k_p)
            new_v.append(v_p)
            o = _prompt_attention(cfg, q_p, k_p, v_p, h)
            t_bias = _expand_rpb(na_rpb[j])
            ck = cache_k[:, j].reshape(nb_lat, cfg.past, d)
            cv = cache_v[:, j].reshape(nb_lat, cfg.past, d)
            o = _latent_attention(cfg, qkv_l, ck, cv, t_bias, o)
            x = _mm_residual(cfg, o, na_w_out[j].astype(BF), x, mods, i, 2)
        x = _moe(cfg, x, norm_ffn_g[i], mods, i, router_grp_w[i], router_grp_b[i], router_exp_w[i],
                 router_exp_b[i], expert_w_gate[i].astype(BF), expert_w_up[i].astype(BF),
                 expert_w_down[i].astype(BF))

    y_prompt = _final_norm(x, final_norm_g, 0, cfg.n_prompt).reshape(x_prompt.shape)
    y_sample = _final_norm(x, final_norm_g, cfg.n_prompt, cfg.n_latent).reshape(x_sample.shape)
    nbp = x_prompt.shape[0]
    kv_shape = (nbp, cfg.prompt_len, cfg.heads, cfg.head_dim)
    new_k = jnp.stack([a.reshape(kv_shape) for a in new_k], axis=1)
    new_v = jnp.stack([a.reshape(kv_shape) for a in new_v], axis=1)
    return (y_prompt, y_sample, new_k, new_v)


def kernel(x_prompt, x_sample, cache_k, cache_v, c, c_ctx, w_ada, b_ada, norm_mix_g, norm_ffn_g,
           fourier_w_in, fourier_w_out, na_w_qkv, na_w_out, na_rpb,
           router_grp_w, router_grp_b, router_exp_w, router_exp_b,
           expert_w_gate, expert_w_up, expert_w_down, final_norm_g):
    bp, lp, d = x_prompt.shape
    bl, ll, _ = x_sample.shape
    heads = na_rpb.shape[1]
    cfg = Cfg(d=d, n_prompt=bp * lp, prompt_len=lp, n_latent=bl * ll, latent_len=ll, heads=heads,
              head_dim=d // heads, past=cache_k.shape[2], ff=expert_w_gate.shape[-1], moe_block=128)
    return _forward(cfg, x_prompt, x_sample, cache_k, cache_v, c, c_ctx, w_ada, b_ada, norm_mix_g,
                    norm_ffn_g, fourier_w_in, fourier_w_out, na_w_qkv, na_w_out, na_rpb,
                    router_grp_w, router_grp_b, router_exp_w, router_exp_b,
                    expert_w_gate, expert_w_up, expert_w_down, final_norm_g)
```

```python
import collections
import functools

import jax
import jax.numpy as jnp
from jax import lax
from jax.experimental import pallas as pl
from jax.experimental.pallas import tpu as pltpu

BF = jnp.bfloat16
F32 = jnp.float32
I32 = jnp.int32

NORM_EPS = 1e-6
MASKED = -1e30
GRID_W = 64
NA_ROWS = 8
NA_COLS = 16
FOURIER_GROUPS = 8
EXPERTS_PER_GROUP = 8
N_EXPERT_GROUPS = 4
N_EXPERTS = N_EXPERT_GROUPS * EXPERTS_PER_GROUP
N_MODS = 6
COND_ROWS = 8
LANES = 128
SUBLANES = 8
SLAB_PAD = 8
ROUTE_LANES = LANES
V7X_VMEM_LIMIT = 56 * 2**20

Cfg = collections.namedtuple(
    "Cfg", "d n_prompt prompt_len n_latent latent_len heads head_dim past ff moe_block")


def _params(sem, vmem=V7X_VMEM_LIMIT):
    return pltpu.CompilerParams(dimension_semantics=sem, vmem_limit_bytes=vmem)


def _silu(x):
    return x / (1.0 + jnp.exp(-x))


def _group_of_tile(i, tm, cfg):
    start = i * tm
    return jnp.where(start < cfg.n_prompt, 0, 1 + (start - cfg.n_prompt) // cfg.latent_len)


def _mod_spec(cfg, layer, which, tm, width=None, col=False):
    width = cfg.d if width is None else width
    base = layer * COND_ROWS * N_MODS + which
    if col:
        return pl.BlockSpec((1, 1, width), lambda i, j: (base + _group_of_tile(i, tm, cfg) * N_MODS, 0, j))
    return pl.BlockSpec((1, 1, width), lambda i: (base + _group_of_tile(i, tm, cfg) * N_MODS, 0, 0))


def _ada_kernel(c_ref, w_ref, b_ref, o_ref):
    s = _silu(c_ref[...]).astype(BF)
    o_ref[0] = jnp.dot(s, w_ref[0].astype(BF), preferred_element_type=F32) + b_ref[0]


def _ada(cond, w_ada, b_ada):
    depth, d, n = w_ada.shape
    tn = min(512, n)
    return pl.pallas_call(
        _ada_kernel,
        grid=(depth, n // tn),
        in_specs=[pl.BlockSpec((COND_ROWS, d), lambda l, j: (0, 0)),
                  pl.BlockSpec((1, d, tn), lambda l, j: (l, 0, j)),
                  pl.BlockSpec((1, 1, tn), lambda l, j: (l, 0, j))],
        out_specs=pl.BlockSpec((1, COND_ROWS, tn), lambda l, j: (l, 0, j)),
        out_shape=jax.ShapeDtypeStruct((depth, COND_ROWS, n), F32),
        compiler_params=_params(("arbitrary", "arbitrary")),
        name="ada",
    )(cond, w_ada, b_ada.reshape(depth, 1, n))


def _norm_mod(x, g, sc, sh):
    y = x * lax.rsqrt(jnp.mean(x * x, axis=-1, keepdims=True) + NORM_EPS) * g
    return y * (1.0 + sc) + sh


def _split_specs(cfg, bm, width, two_d):
    pt = cfg.n_prompt // bm
    if two_d:
        return pt, [pl.BlockSpec((bm, width), lambda i, j: (jnp.minimum(i, pt - 1), j)),
                    pl.BlockSpec((bm, width), lambda i, j: (jnp.maximum(i - pt, 0), j))]
    return pt, [pl.BlockSpec((bm, width), lambda i: (jnp.minimum(i, pt - 1), 0)),
                pl.BlockSpec((bm, width), lambda i: (jnp.maximum(i - pt, 0), 0))]


def _nm_kernel(xp_ref, xl_ref, g_ref, sc_ref, sh_ref, o_ref, *, prompt_tiles):
    i = pl.program_id(0)

    @pl.when(i < prompt_tiles)
    def _prompt():
        o_ref[...] = _norm_mod(xp_ref[...], g_ref[...], sc_ref[0], sh_ref[0]).astype(o_ref.dtype)

    @pl.when(i >= prompt_tiles)
    def _latent():
        o_ref[...] = _norm_mod(xl_ref[...], g_ref[...], sc_ref[0], sh_ref[0]).astype(o_ref.dtype)


def _nm(cfg, xp, xl, g, mods, layer, sc_k, sh_k):
    d = cfg.d
    t = cfg.n_prompt + cfg.n_latent
    tm = 512
    pt, x_specs = _split_specs(cfg, tm, d, False)
    return pl.pallas_call(
        functools.partial(_nm_kernel, prompt_tiles=pt),
        grid=(t // tm,),
        in_specs=x_specs + [pl.BlockSpec((1, d), lambda i: (0, 0)),
                            _mod_spec(cfg, layer, sc_k, tm),
                            _mod_spec(cfg, layer, sh_k, tm)],
        out_specs=pl.BlockSpec((tm, d), lambda i: (i, 0)),
        out_shape=jax.ShapeDtypeStruct((t, d), BF),
        compiler_params=_params(("arbitrary",)),
        name="norm_mod",
    )(xp, xl, g.reshape(1, d), mods, mods)


def _mm_kernel(x_ref, w_ref, o_ref, *, n_scaled, scale):
    acc = jnp.dot(x_ref[...], w_ref[...], preferred_element_type=F32)
    if n_scaled:
        acc = acc * jnp.where(pl.program_id(1) < n_scaled, scale, 1.0)
    o_ref[...] = acc.astype(o_ref.dtype)


def _mm(x, w, *, row0, rows, col0, cols, out_dtype, scaled_cols=0, scale=1.0):
    k = x.shape[1]
    bm, bn = min(1024, rows), min(512, cols)
    r0, c0 = row0 // bm, col0 // bn
    return pl.pallas_call(
        functools.partial(_mm_kernel, n_scaled=scaled_cols // bn, scale=scale),
        grid=(rows // bm, cols // bn),
        in_specs=[pl.BlockSpec((bm, k), lambda i, j: (r0 + i, 0)),
                  pl.BlockSpec((k, bn), lambda i, j: (0, c0 + j))],
        out_specs=pl.BlockSpec((bm, bn), lambda i, j: (i, j)),
        out_shape=jax.ShapeDtypeStruct((rows, cols), out_dtype),
        compiler_params=_params(("arbitrary", "arbitrary")),
        name="matmul",
    )(x, w)


def _mr_kernel(x_ref, w_ref, r_ref, g_ref, o_ref):
    acc = jnp.dot(x_ref[...], w_ref[...], preferred_element_type=F32)
    o_ref[...] = r_ref[...] + g_ref[0] * acc


def _mr_split_kernel(x_ref, w_ref, rp_ref, rl_ref, g_ref, o_ref, *, prompt_tiles):
    acc = g_ref[0] * jnp.dot(x_ref[...], w_ref[...], preferred_element_type=F32)
    i = pl.program_id(0)

    @pl.when(i < prompt_tiles)
    def _prompt():
        o_ref[...] = rp_ref[...] + acc

    @pl.when(i >= prompt_tiles)
    def _latent():
        o_ref[...] = rl_ref[...] + acc


def _mm_residual(cfg, x, w, res, mods, layer, gate_k):
    t, k = x.shape
    n = w.shape[1]
    bm, bn = 1024, min(512, n)
    common = dict(
        grid=(t // bm, n // bn),
        out_specs=pl.BlockSpec((bm, bn), lambda i, j: (i, j)),
        out_shape=jax.ShapeDtypeStruct((t, n), F32),
        compiler_params=_params(("arbitrary", "arbitrary")),
        name="matmul_residual",
    )
    xw_specs = [pl.BlockSpec((bm, k), lambda i, j: (i, 0)), pl.BlockSpec((k, bn), lambda i, j: (0, j))]
    gate_spec = _mod_spec(cfg, layer, gate_k, bm, width=bn, col=True)
    if isinstance(res, tuple):
        pt, r_specs = _split_specs(cfg, bm, bn, True)
        return pl.pallas_call(
            functools.partial(_mr_split_kernel, prompt_tiles=pt),
            in_specs=xw_specs + r_specs + [gate_spec], **common,
        )(x, w, res[0], res[1], mods)
    return pl.pallas_call(
        _mr_kernel,
        in_specs=xw_specs + [pl.BlockSpec((bm, bn), lambda i, j: (i, j)), gate_spec],
        input_output_aliases={2: 0}, **common,
    )(x, w, res, mods)


def _fa_kernel(x_ref, w_ref, cs_ref, o_ref):
    u = jnp.dot(x_ref[...], w_ref[...], preferred_element_type=F32).astype(BF)
    v = jnp.dot(u, cs_ref[...], preferred_element_type=F32)
    n = u.shape[1]
    o_ref[0] = v[:, :n].astype(BF)
    o_ref[1] = v[:, n:].astype(BF)


def _fourier_in(h, w_in, cs):
    t, d = h.shape
    gd = d // FOURIER_GROUPS
    bm = 512
    return pl.pallas_call(
        _fa_kernel,
        grid=(t // bm, FOURIER_GROUPS),
        in_specs=[pl.BlockSpec((bm, d), lambda i, j: (i, 0)),
                  pl.BlockSpec((d, gd), lambda i, j: (0, j)),
                  pl.BlockSpec((gd, 2 * gd), lambda i, j: (0, 0))],
        out_specs=pl.BlockSpec((2, bm, gd), lambda i, j: (0, i, j)),
        out_shape=jax.ShapeDtypeStruct((2, t, d), BF),
        compiler_params=_params(("arbitrary", "arbitrary")),
        name="fourier_in",
    )(h, w_in, cs)


def _fb_kernel(ac_ref, as_ref, vc_ref, vs_ref, prev_ref, o_ref):
    del prev_ref
    acc = jnp.dot(ac_ref[0], vc_ref[0], preferred_element_type=F32)
    acc = acc + jnp.dot(as_ref[0], vs_ref[0], preferred_element_type=F32)
    o_ref[...] = acc.astype(o_ref.dtype)


def _fourier_seq(a, v, prev, *, row0, n_seq, seq_len, bm, bn):
    _, t, d = v.shape
    rb0, ob0 = row0 // seq_len, row0 // bm
    ni = seq_len // bm
    return pl.pallas_call(
        _fb_kernel,
        grid=(n_seq, d // bn, ni),
        in_specs=[pl.BlockSpec((1, bm, seq_len), lambda b, j, i: (0, i, 0)),
                  pl.BlockSpec((1, bm, seq_len), lambda b, j, i: (1, i, 0)),
                  pl.BlockSpec((1, seq_len, bn), lambda b, j, i: (0, rb0 + b, j)),
                  pl.BlockSpec((1, seq_len, bn), lambda b, j, i: (1, rb0 + b, j)),
                  pl.BlockSpec(memory_space=pl.ANY)],
        out_specs=pl.BlockSpec((bm, bn), lambda b, j, i: (ob0 + b * ni + i, j)),
        out_shape=jax.ShapeDtypeStruct((t, d), BF),
        input_output_aliases={4: 0},
        compiler_params=_params(("arbitrary", "arbitrary", "arbitrary")),
        name="fourier_seq",
    )(a, a, v, v, prev)


def _dft_tables(n):
    j = lax.broadcasted_iota(I32, (n, n), 0)
    k = lax.broadcasted_iota(I32, (n, n), 1)
    ang = ((j * k) % n).astype(F32) * (2.0 * jnp.pi / n)
    return jnp.cos(ang), jnp.sin(ang)


def _dft_tables_big(n, r=64):
    a = lax.broadcasted_iota(I32, (r, n), 0)
    k = lax.broadcasted_iota(I32, (r, n), 1)
    ang_hi = ((a * k) % r).astype(F32) * (2.0 * jnp.pi / r)
    ang_lo = ((a * k) % n).astype(F32) * (2.0 * jnp.pi / n)
    ch, sh = jnp.cos(ang_hi)[:, None, :], jnp.sin(ang_hi)[:, None, :]
    cl, sl = jnp.cos(ang_lo)[None, :, :], jnp.sin(ang_lo)[None, :, :]
    cos = (ch * cl - sh * sl).reshape(n, n)
    sin = (sh * cl + ch * sl).reshape(n, n)
    return cos, sin


def _fourier_mix(cfg, h, w_in):
    d = cfg.d
    gd = d // FOURIER_GROUPS
    cc, sc = _dft_tables(gd)
    cs = (jnp.concatenate([cc, sc], axis=1) * gd ** -0.5).astype(BF)
    v = _fourier_in(h, w_in, cs)
    cp, sp = _dft_tables(cfg.prompt_len)
    a_p = (jnp.stack([cp, -sp]) * cfg.prompt_len ** -0.5).astype(BF)
    cl, sl = _dft_tables_big(cfg.latent_len)
    a_l = (jnp.stack([cl, -sl]) * cfg.latent_len ** -0.5).astype(BF)
    f = _fourier_seq(a_p, v, h, row0=0, n_seq=cfg.n_prompt // cfg.prompt_len,
                     seq_len=cfg.prompt_len, bm=cfg.prompt_len, bn=d)
    f = _fourier_seq(a_l, v, f, row0=cfg.n_prompt, n_seq=cfg.n_latent // cfg.latent_len,
                     seq_len=cfg.latent_len, bm=512, bn=512)
    return f


def _softmax_pv(scores, values):
    m = functools.reduce(jnp.maximum, [jnp.max(s, axis=-1, keepdims=True) for s in scores])
    ps = [jnp.exp(s - m) for s in scores]
    l = functools.reduce(jnp.add, [jnp.sum(p, axis=-1, keepdims=True) for p in ps])
    o = functools.reduce(jnp.add, [jnp.dot(p.astype(BF), v, preferred_element_type=F32)
                                   for p, v in zip(ps, values)])
    return o / l


def _qk(q, k):
    return lax.dot_general(q, k, (((1,), (1,)), ((), ())), preferred_element_type=F32)


def _pa_kernel(q_ref, k_ref, v_ref, *rest, heads, dh):
    o_ref = rest[-1]
    for h in range(heads):
        sl = slice(h * dh, (h + 1) * dh)
        s = _qk(q_ref[:, sl], k_ref[:, sl].astype(BF))
        o_ref[:, sl] = _softmax_pv([s], [v_ref[:, sl].astype(BF)]).astype(o_ref.dtype)


def _prompt_attention(cfg, q, k, v, prev):
    lp, d = cfg.prompt_len, cfg.d
    hb = min(8, cfg.heads)
    w = hb * cfg.head_dim
    spec = pl.BlockSpec((lp, w), lambda b, g: (b, g))
    return pl.pallas_call(
        functools.partial(_pa_kernel, heads=hb, dh=cfg.head_dim),
        grid=(cfg.n_prompt // lp, d // w),
        in_specs=[spec, spec, spec, pl.BlockSpec(memory_space=pl.ANY)],
        out_specs=spec,
        out_shape=jax.ShapeDtypeStruct(prev.shape, BF),
        input_output_aliases={3: 0},
        compiler_params=_params(("arbitrary", "arbitrary")),
        name="prompt_attention",
    )(q, k, v, prev)


def _rx_kernel(r_ref, o_ref):
    r = r_ref[...]
    hi = r.astype(BF)
    r1 = r - hi.astype(F32)
    mid = r1.astype(BF)
    lo = (r1 - mid.astype(F32)).astype(BF)
    n = o_ref.shape[1]
    idx = lax.broadcasted_iota(I32, (ROUTE_LANES, n), 1)
    row = lax.broadcasted_iota(I32, (ROUTE_LANES, n), 0)
    qc, kc = idx >> 7, idx & (GRID_W - 1)
    dc = jnp.clip(kc - qc + NA_COLS - 1, 0, 2 * NA_COLS - 2)
    onehot = jnp.where(row == dc, 1.0, 0.0).astype(BF)
    t = (jnp.dot(hi, onehot, preferred_element_type=F32)
         + jnp.dot(mid, onehot, preferred_element_type=F32)
         + jnp.dot(lo, onehot, preferred_element_type=F32))
    col = lax.broadcasted_iota(I32, (1, n), 1)
    qc1, kc1 = col >> 7, col & (GRID_W - 1)
    c0 = jnp.clip(qc1 - NA_COLS // 2, 0, GRID_W - NA_COLS)
    o_ref[...] = jnp.where((kc1 >= c0) & (kc1 < c0 + NA_COLS), t, MASKED)


def _expand_rpb(rpb):
    heads, nr, nc = rpb.shape
    rows = heads * nr
    tr = rows // 4
    r2 = jnp.pad(rpb.reshape(rows, nc), ((0, 0), (0, ROUTE_LANES - nc)))
    out = pl.pallas_call(
        _rx_kernel,
        grid=(rows // tr,),
        in_specs=[pl.BlockSpec((tr, ROUTE_LANES), lambda i: (i, 0))],
        out_specs=pl.BlockSpec((tr, GRID_W * 2 * GRID_W), lambda i: (i, 0)),
        out_shape=jax.ShapeDtypeStruct((rows, GRID_W * 2 * GRID_W), F32),
        compiler_params=_params(("arbitrary",)),
        name="rpb_expand",
    )(r2)
    return out.reshape(heads, nr, GRID_W, 2 * GRID_W)


NA_QROWS = 4
NA_KROWS = NA_QROWS + NA_ROWS
NA_VARIANTS = ((0, 0), (2 * NA_QROWS, NA_QROWS), (GRID_W - NA_QROWS, GRID_W - NA_KROWS))


def _na_kernel(q_ref, k_ref, v_ref, kc_ref, vc_ref, t_ref, *rest):
    o_ref, bias_ref = rest[-2], rest[-1]
    w = GRID_W

    @pl.when(pl.program_id(1) == 0)
    def _build_bias():
        left = lax.broadcasted_iota(I32, (w, 2 * w), 1) < w
        masked = jnp.full((w, 2 * w), MASKED, F32)

        def table(r, kr):
            rs = min(max(r - NA_ROWS // 2, 0), w - NA_ROWS)
            return t_ref[0, kr - r + NA_ROWS - 1] if rs <= kr < rs + NA_ROWS else masked

        for var, (r0, k0) in enumerate(NA_VARIANTS):
            for i in range(NA_QROWS):
                for j in range(0, NA_KROWS, 2):
                    pair = jnp.where(left, table(r0 + i, k0 + j), table(r0 + i, k0 + j + 1))
                    bias_ref[var, i * w:(i + 1) * w, j * w:(j + 2) * w] = pair

    kc = kc_ref[0].astype(BF)
    vc = vc_ref[0].astype(BF)
    nq = NA_QROWS * w
    nk = NA_KROWS * w
    n_blocks = w // NA_QROWS

    def block(qb, carry):
        var = jnp.where(qb == 0, 0, jnp.where(qb == n_blocks - 1, 2, 1))
        k0 = jnp.clip(qb * NA_QROWS - NA_ROWS // 2, 0, w - NA_KROWS)
        q0 = pl.multiple_of(qb * nq, nq)
        ks = pl.multiple_of(k0 * w, w)
        q = q_ref[pl.ds(q0, nq), :]
        s_lat = _qk(q, k_ref[pl.ds(ks, nk), :]) + bias_ref[var]
        s_ctx = _qk(q, kc)
        o = _softmax_pv([s_lat, s_ctx], [v_ref[pl.ds(ks, nk), :], vc])
        o_ref[pl.ds(q0, nq), :] = o.astype(o_ref.dtype)
        return carry

    lax.fori_loop(0, n_blocks, block, 0)


def _latent_attention(cfg, qkv, cache_k, cache_v, t_bias, prev):
    d, dh, heads, ls = cfg.d, cfg.head_dim, cfg.heads, cfg.latent_len
    nb = cfg.n_latent // ls
    rb0 = cfg.n_prompt // ls
    w = GRID_W
    return pl.pallas_call(
        _na_kernel,
        grid=(heads, nb),
        in_specs=[pl.BlockSpec((ls, dh), lambda h, b: (b, h)),
                  pl.BlockSpec((ls, dh), lambda h, b: (b, heads + h)),
                  pl.BlockSpec((ls, dh), lambda h, b: (b, 2 * heads + h)),
                  pl.BlockSpec((1, cfg.past, dh), lambda h, b: (b, 0, h)),
                  pl.BlockSpec((1, cfg.past, dh), lambda h, b: (b, 0, h)),
                  pl.BlockSpec((1, 2 * NA_ROWS - 1, w, 2 * w), lambda h, b: (h, 0, 0, 0)),
                  pl.BlockSpec(memory_space=pl.ANY)],
        out_specs=pl.BlockSpec((ls, dh), lambda h, b: (rb0 + b, h)),
        out_shape=jax.ShapeDtypeStruct(prev.shape, BF),
        scratch_shapes=[pltpu.VMEM((len(NA_VARIANTS), NA_QROWS * w, NA_KROWS * w), F32)],
        input_output_aliases={6: 0},
        compiler_params=_params(("arbitrary", "arbitrary")),
        name="latent_attention",
    )(qkv, qkv, qkv, cache_k, cache_v, t_bias, prev)


U32 = jnp.uint32
HIGH_HALF = 0xFFFF0000


def _slab(d):
    rows = d // (2 * LANES)
    return rows, -(-rows // SUBLANES) * SUBLANES + SLAB_PAD


def _bf16_bits(x):
    b = lax.bitcast_convert_type(x, U32)
    return b + jnp.uint32(0x7FFF) + ((b >> 16) & jnp.uint32(1))


def _store_slabs(ref, x):
    n, d = x.shape
    rows, pitch = _slab(d)
    for r in range(rows):
        lo = x[:, r * LANES:(r + 1) * LANES]
        hi = x[:, d // 2 + r * LANES:d // 2 + (r + 1) * LANES]
        ref[pl.ds(r, n, stride=pitch), :] = (_bf16_bits(hi) & jnp.uint32(HIGH_HALF)) | (_bf16_bits(lo) >> 16)
    for r in range(rows, pitch):
        ref[pl.ds(r, n, stride=pitch), :] = jnp.zeros((n, LANES), U32)


def _load_slab_piece(ref, lead, n, pitch, r):
    w = ref[lead + (pl.ds(r, n, stride=pitch), slice(None))]
    return (lax.bitcast_convert_type(w << 16, F32),
            lax.bitcast_convert_type(w & jnp.uint32(HIGH_HALF), F32))


def _nmr_kernel(x_ref, g_ref, sc_ref, sh_ref, wr_ref, br_ref, hs_ref, ri_ref, rg_ref, cnt_ref, base_ref):
    tm, d = x_ref.shape

    @pl.when(pl.program_id(0) == 0)
    def _init():
        base_ref[...] = jnp.zeros_like(base_ref)

    h = _norm_mod(x_ref[...], g_ref[...], sc_ref[0], sh_ref[0])
    _store_slabs(hs_ref, h)

    h_hi = h.astype(BF)
    h_lo = (h - h_hi.astype(F32)).astype(BF)
    wr = wr_ref[...]
    w_hi = wr.astype(BF)
    w_lo = (wr - w_hi.astype(F32)).astype(BF)
    logits = (jnp.dot(h_hi, w_hi, preferred_element_type=F32)
              + jnp.dot(h_lo, w_hi, preferred_element_type=F32)
              + jnp.dot(h_hi, w_lo, preferred_element_type=F32)) + br_ref[...]

    lane = lax.broadcasted_iota(I32, (tm, ROUTE_LANES), 1)
    lane_f = lane.astype(F32)

    def first_argmax(v):
        m = jnp.max(v, axis=-1, keepdims=True)
        first = jnp.min(jnp.where(v == m, lane_f, float(ROUTE_LANES)), axis=-1, keepdims=True)
        return m, first.astype(I32)

    gl = jnp.where(lane < N_EXPERT_GROUPS, logits, MASKED)
    gmax, gi = first_argmax(gl)
    g_w = 1.0 / jnp.sum(jnp.exp(gl - gmax), axis=-1, keepdims=True)
    e0 = N_EXPERT_GROUPS + gi * EXPERTS_PER_GROUP
    el = jnp.where((lane >= e0) & (lane < e0 + EXPERTS_PER_GROUP), logits, MASKED)
    m1, i1 = first_argmax(el)
    m2, i2 = first_argmax(jnp.where(lane == i1, MASKED, el))
    tt = jnp.exp(m2 - m1)
    w1 = 1.0 / (1.0 + tt)
    eid0, eid1 = i1 - N_EXPERT_GROUPS, i2 - N_EXPERT_GROUPS

    oh = jnp.where((lane == eid0) | (lane == eid1), 1.0, 0.0)
    row = lax.broadcasted_iota(I32, (tm, tm), 0)
    colt = lax.broadcasted_iota(I32, (tm, tm), 1)
    earlier = jnp.where(row > colt, 1.0, 0.0).astype(BF)
    before = jnp.dot(earlier, oh.astype(BF), preferred_element_type=F32) + base_ref[0:1, :]
    rank0 = jnp.sum(jnp.where(lane == eid0, before, 0.0), axis=-1, keepdims=True).astype(I32)
    rank1 = jnp.sum(jnp.where(lane == eid1, before, 0.0), axis=-1, keepdims=True).astype(I32)
    base_ref[...] = base_ref[...] + jnp.sum(oh, axis=0, keepdims=True)

    ri_ref[...] = jnp.where(lane == 0, eid0, jnp.where(lane == 1, eid1,
                            jnp.where(lane == 2, rank0, jnp.where(lane == 3, rank1, 0))))
    rg_ref[...] = jnp.where(lane == 0, g_w * w1, jnp.where(lane == 1, g_w * (tt * w1), 0.0))
    cnt_ref[...] = base_ref[...]


def _norm_route(cfg, x, g, mods, layer, w_route, b_route):
    t, d = x.shape
    tm = 256
    _, pitch = _slab(d)
    return pl.pallas_call(
        _nmr_kernel,
        grid=(t // tm,),
        in_specs=[pl.BlockSpec((tm, d), lambda i: (i, 0)),
                  pl.BlockSpec((1, d), lambda i: (0, 0)),
                  _mod_spec(cfg, layer, 4, tm),
                  _mod_spec(cfg, layer, 3, tm),
                  pl.BlockSpec((d, ROUTE_LANES), lambda i: (0, 0)),
                  pl.BlockSpec((1, ROUTE_LANES), lambda i: (0, 0))],
        out_specs=[pl.BlockSpec((tm * pitch, LANES), lambda i: (i, 0)),
                   pl.BlockSpec((tm, ROUTE_LANES), lambda i: (i, 0)),
                   pl.BlockSpec((tm, ROUTE_LANES), lambda i: (i, 0)),
                   pl.BlockSpec((SUBLANES, ROUTE_LANES), lambda i: (0, 0))],
        out_shape=[jax.ShapeDtypeStruct((t * pitch, LANES), U32),
                   jax.ShapeDtypeStruct((t, ROUTE_LANES), I32),
                   jax.ShapeDtypeStruct((t, ROUTE_LANES), F32),
                   jax.ShapeDtypeStruct((SUBLANES, ROUTE_LANES), F32)],
        scratch_shapes=[pltpu.VMEM((SUBLANES, ROUTE_LANES), F32)],
        compiler_params=_params(("arbitrary",)),
        name="norm_route",
    )(x, g.reshape(1, d), mods, mods, w_route, b_route)


MOVE_TOKENS = 256


def _wait_bytes_of(buf_ref, sem):
    pltpu.make_async_copy(buf_ref, buf_ref, sem).wait()


def _ds_kernel(dest_ref, hs_ref, init_ref, xs_ref, sem):
    del init_ref
    n = MOVE_TOKENS
    pitch = hs_ref.shape[0] // n
    rows = pitch - SLAB_PAD

    def issue(tk, carry):
        src = hs_ref.at[pl.ds(pl.multiple_of(tk * pitch, SUBLANES), rows), :]
        for k in range(2):
            pltpu.make_async_copy(src, xs_ref.at[dest_ref[0, 0, 2 * tk + k], pl.ds(0, rows), :], sem).start()
        return carry

    lax.fori_loop(0, n, issue, 0)
    _wait_bytes_of(xs_ref.at[pl.ds(0, 2 * n), pl.ds(0, rows), :], sem)


def _dispatch(hs, dest, n_slots):
    n = MOVE_TOKENS
    t = dest.shape[0] // 2
    pitch = hs.shape[0] // t
    return pl.pallas_call(
        _ds_kernel,
        grid=(t // n,),
        in_specs=[pl.BlockSpec((1, 1, 2 * n), lambda i: (i, 0, 0), memory_space=pltpu.SMEM),
                  pl.BlockSpec((n * pitch, LANES), lambda i: (i, 0)),
                  pl.BlockSpec(memory_space=pl.ANY)],
        out_specs=pl.BlockSpec(memory_space=pl.ANY),
        out_shape=jax.ShapeDtypeStruct((n_slots, pitch, LANES), U32),
        scratch_shapes=[pltpu.SemaphoreType.DMA],
        input_output_aliases={2: 0},
        compiler_params=_params(("arbitrary",)),
        name="moe_dispatch",
    )(dest.reshape(t // n, 1, 2 * n), hs, jnp.zeros((n_slots, pitch, LANES), U32))


def _ex_kernel(blk_e_ref, n_used_ref, x_ref, wg_ref, wu_ref, wd_ref, o_ref, xb_ref):
    del blk_e_ref
    bm, d = xb_ref.shape
    rows, pitch = _slab(d)

    @pl.when(pl.program_id(0) < n_used_ref[0])
    def _compute():
        for r in range(rows):
            lo, hi = _load_slab_piece(x_ref, (), bm, pitch, r)
            xb_ref[:, r * LANES:(r + 1) * LANES] = lo.astype(BF)
            xb_ref[:, d // 2 + r * LANES:d // 2 + (r + 1) * LANES] = hi.astype(BF)
        x = xb_ref[...]
        g = jnp.dot(x, wg_ref[0], preferred_element_type=F32)
        u = jnp.dot(x, wu_ref[0], preferred_element_type=F32)
        y = jnp.dot((_silu(g) * u).astype(BF), wd_ref[0], preferred_element_type=F32)
        _store_slabs(o_ref, y)


def _experts(cfg, xs, blk_e, n_used, wg, wu, wd):
    n_slots, sr, sw = xs.shape
    d, ff, bm = cfg.d, cfg.ff, cfg.moe_block
    nb = n_slots // bm

    def blk(b, be, nu):
        return jnp.minimum(b, nu[0] - 1)

    grid_spec = pltpu.PrefetchScalarGridSpec(
        num_scalar_prefetch=2,
        grid=(nb,),
        in_specs=[pl.BlockSpec((bm * sr, sw), lambda b, be, nu: (blk(b, be, nu), 0)),
                  pl.BlockSpec((1, d, ff), lambda b, be, nu: (be[blk(b, be, nu)], 0, 0)),
                  pl.BlockSpec((1, d, ff), lambda b, be, nu: (be[blk(b, be, nu)], 0, 0)),
                  pl.BlockSpec((1, ff, d), lambda b, be, nu: (be[blk(b, be, nu)], 0, 0))],
        out_specs=pl.BlockSpec((bm * sr, sw), lambda b, be, nu: (blk(b, be, nu), 0)),
        scratch_shapes=[pltpu.VMEM((bm, d), BF)],
    )
    ys = pl.pallas_call(
        _ex_kernel,
        grid_spec=grid_spec,
        out_shape=jax.ShapeDtypeStruct((n_slots * sr, sw), U32),
        input_output_aliases={2: 0},
        compiler_params=_params(("arbitrary",)),
        name="moe_experts",
    )(blk_e, n_used, xs.reshape(n_slots * sr, sw), wg, wu, wd)
    return ys.reshape(n_slots, sr, sw)


def _gather_combine(dcur_ref, dnext_ref, x_ref, ys_ref, rg_ref, gt_ref, o_ref, ybuf_ref, sems):
    tm, d = x_ref.shape
    rows, pitch = _slab(d)
    copy_rows = pitch - SLAB_PAD
    i, n_steps = pl.program_id(0), pl.num_programs(0)

    def start_gather(dref, slot):
        def issue(tk, carry):
            for k in range(2):
                row0 = pl.multiple_of((k * tm + tk) * pitch, SUBLANES)
                pltpu.make_async_copy(ys_ref.at[dref[0, 0, 2 * tk + k], pl.ds(0, copy_rows), :],
                                      ybuf_ref.at[slot, pl.ds(row0, copy_rows), :], sems.at[slot]).start()
            return carry
        lax.fori_loop(0, tm, issue, 0)

    @pl.when(i == 0)
    def _first():
        start_gather(dcur_ref, 0)

    @pl.when(i + 1 < n_steps)
    def _next():
        start_gather(dnext_ref, (i + 1) % 2)

    slot = i % 2
    _wait_bytes_of(ybuf_ref.at[slot, pl.ds(0, 2 * tm * copy_rows), :], sems.at[slot])
    g0, g1 = rg_ref[:, 0:1], rg_ref[:, 1:2]
    for r in range(rows):
        lo0, hi0 = _load_slab_piece(ybuf_ref, (slot,), tm, pitch, r)
        lo1, hi1 = _load_slab_piece(ybuf_ref, (slot,), tm, pitch, tm * pitch + r)
        for c0, y0, y1 in ((r * LANES, lo0, lo1), (d // 2 + r * LANES, hi0, hi1)):
            sl = slice(c0, c0 + LANES)
            o_ref[:, sl] = x_ref[:, sl] + gt_ref[0][:, sl] * (y0 * g0 + y1 * g1)


def _cb_next_kernel(dcur_ref, dnext_ref, x_ref, ys_ref, rg_ref, gt_ref, g_ref, sc_ref, sh_ref,
                    o_ref, hn_ref, ybuf_ref, sems):
    _gather_combine(dcur_ref, dnext_ref, x_ref, ys_ref, rg_ref, gt_ref, o_ref, ybuf_ref, sems)
    hn_ref[...] = _norm_mod(o_ref[...], g_ref[...], sc_ref[0], sh_ref[0]).astype(hn_ref.dtype)


def _cb_final_kernel(dcur_ref, dnext_ref, x_ref, ys_ref, rg_ref, gt_ref, g_ref,
                     yp_ref, yl_ref, ybuf_ref, sems, xn_ref, *, prompt_tiles):
    _gather_combine(dcur_ref, dnext_ref, x_ref, ys_ref, rg_ref, gt_ref, xn_ref, ybuf_ref, sems)
    xn = xn_ref[...]
    y = xn * lax.rsqrt(jnp.mean(xn * xn, axis=-1, keepdims=True) + NORM_EPS) * g_ref[...]
    i = pl.program_id(0)

    @pl.when(i < prompt_tiles)
    def _prompt():
        yp_ref[...] = y

    @pl.when(i >= prompt_tiles)
    def _latent():
        yl_ref[...] = y


def _combine(cfg, x, ys, dest, rg, mods, layer, g, next_layer):
    t, d = x.shape
    tm = MOVE_TOKENS
    _, pitch = _slab(d)
    n = t // tm
    dest3 = dest.reshape(n, 1, 2 * tm)
    row = pl.BlockSpec((tm, d), lambda i: (i, 0))
    in_specs = [pl.BlockSpec((1, 1, 2 * tm), lambda i: (i, 0, 0), memory_space=pltpu.SMEM),
                pl.BlockSpec((1, 1, 2 * tm), lambda i: (jnp.minimum(i + 1, n - 1), 0, 0),
                             memory_space=pltpu.SMEM),
                row,
                pl.BlockSpec(memory_space=pl.ANY),
                pl.BlockSpec((tm, ROUTE_LANES), lambda i: (i, 0)),
                _mod_spec(cfg, layer, 5, tm),
                pl.BlockSpec((1, d), lambda i: (0, 0))]
    scratch = [pltpu.VMEM((2, 2 * tm * pitch, LANES), U32), pltpu.SemaphoreType.DMA((2,))]
    args = [dest3, dest3, x, ys, rg, mods, g.reshape(1, d)]
    if next_layer is not None:
        return pl.pallas_call(
            _cb_next_kernel,
            grid=(n,),
            in_specs=in_specs + [_mod_spec(cfg, next_layer, 1, tm), _mod_spec(cfg, next_layer, 0, tm)],
            out_specs=[row, row],
            out_shape=[jax.ShapeDtypeStruct((t, d), F32), jax.ShapeDtypeStruct((t, d), BF)],
            scratch_shapes=scratch,
            input_output_aliases={2: 0},
            compiler_params=_params(("arbitrary",)),
            name="moe_combine_norm",
        )(*args, mods, mods)
    pt = cfg.n_prompt // tm
    return pl.pallas_call(
        functools.partial(_cb_final_kernel, prompt_tiles=pt),
        grid=(n,),
        in_specs=in_specs,
        out_specs=[pl.BlockSpec((tm, d), lambda i: (jnp.minimum(i, pt - 1), 0)),
                   pl.BlockSpec((tm, d), lambda i: (jnp.maximum(i - pt, 0), 0))],
        out_shape=[jax.ShapeDtypeStruct((cfg.n_prompt, d), F32), jax.ShapeDtypeStruct((cfg.n_latent, d), F32)],
        scratch_shapes=scratch + [pltpu.VMEM((tm, d), F32)],
        compiler_params=_params(("arbitrary",)),
        name="moe_combine_final",
    )(*args)


def _moe(cfg, x, g, mods, layer, w_grp, b_grp, w_exp, b_exp, wg, wu, wd, g_after, next_layer):
    t, d = x.shape
    bm = cfg.moe_block
    pad = ROUTE_LANES - N_EXPERT_GROUPS - N_EXPERTS
    w_route = jnp.pad(jnp.concatenate([w_grp, w_exp], axis=1), ((0, 0), (0, pad)))
    b_route = jnp.pad(jnp.concatenate([b_grp, b_exp]), (0, pad)).reshape(1, ROUTE_LANES)
    hs, ri, rg, cnt = _norm_route(cfg, x, g, mods, layer, w_route, b_route)

    counts = cnt[0, :N_EXPERTS].astype(I32)
    padded = (counts + bm - 1) // bm * bm
    ends = jnp.cumsum(padded)
    starts = ends - padded
    eid, rank = ri[:, 0:2], ri[:, 2:4]
    start_of = jnp.sum(jnp.where(eid[:, :, None] == jnp.arange(N_EXPERTS, dtype=I32), starts, 0), axis=-1)
    dest = (start_of + rank).reshape(-1)
    n_blocks = -(-2 * t // bm) + N_EXPERTS
    blk_e = jnp.minimum(jnp.sum(ends[None, :] <= (jnp.arange(n_blocks, dtype=I32) * bm)[:, None], axis=1),
                        N_EXPERTS - 1).astype(I32) + layer * N_EXPERTS
    n_used = (ends[-1:] // bm).astype(I32)

    xs = _dispatch(hs, dest, n_blocks * bm)
    ys = _experts(cfg, xs, blk_e, n_used, wg, wu, wd)
    return _combine(cfg, x, ys, dest, rg, mods, layer, g_after, next_layer)


def _forward(cfg, x_prompt, x_sample, cache_k, cache_v, c, c_ctx, w_ada, b_ada, norm_mix_g, norm_ffn_g,
             fourier_w_in, fourier_w_out, na_w_qkv, na_w_out, na_rpb,
             router_grp_w, router_grp_b, router_exp_w, router_exp_b,
             expert_w_gate, expert_w_up, expert_w_down, final_norm_g):
    d = cfg.d
    depth = w_ada.shape[0]
    xp, xl = x_prompt.reshape(cfg.n_prompt, d), x_sample.reshape(cfg.n_latent, d)

    nb_lat = cfg.n_latent // cfg.latent_len
    cond = jnp.concatenate([c_ctx[None], c, jnp.zeros((COND_ROWS - 1 - nb_lat, d), F32)], axis=0)
    mods = _ada(cond, w_ada, b_ada).reshape(depth * COND_ROWS * N_MODS, 1, d)

    n_all = depth * N_EXPERTS
    wg_all = expert_w_gate.astype(BF).reshape(n_all, d, cfg.ff)
    wu_all = expert_w_up.astype(BF).reshape(n_all, d, cfg.ff)
    wd_all = expert_w_down.astype(BF).reshape(n_all, cfg.ff, d)

    x = (xp, xl)
    h = _nm(cfg, xp, xl, norm_mix_g[0], mods, 0, 1, 0)
    new_k, new_v = [], []
    for i in range(depth):
        j = i // 2
        if i % 2 == 0:
            f = _fourier_mix(cfg, h, fourier_w_in[j].astype(BF))
            x = _mm_residual(cfg, f, fourier_w_out[j].astype(BF), x, mods, i, 2)
        else:
            w_qkv = na_w_qkv[j].astype(BF)
            scale = cfg.head_dim ** -0.5
            q_p = _mm(h, w_qkv, row0=0, rows=cfg.n_prompt, col0=0, cols=d, out_dtype=BF,
                      scaled_cols=d, scale=scale)
            k_p = _mm(h, w_qkv, row0=0, rows=cfg.n_prompt, col0=d, cols=d, out_dtype=F32)
            v_p = _mm(h, w_qkv, row0=0, rows=cfg.n_prompt, col0=2 * d, cols=d, out_dtype=F32)
            qkv_l = _mm(h, w_qkv, row0=cfg.n_prompt, rows=cfg.n_latent, col0=0, cols=3 * d, out_dtype=BF,
                        scaled_cols=d, scale=scale)
            new_k.append(k_p)
            new_v.append(v_p)
            o = _prompt_attention(cfg, q_p, k_p, v_p, h)
            t_bias = _expand_rpb(na_rpb[j])
            ck = cache_k[:, j].reshape(nb_lat, cfg.past, d)
            cv = cache_v[:, j].reshape(nb_lat, cfg.past, d)
            o = _latent_attention(cfg, qkv_l, ck, cv, t_bias, o)
            x = _mm_residual(cfg, o, na_w_out[j].astype(BF), x, mods, i, 2)
        last = i == depth - 1
        x, h = _moe(cfg, x, norm_ffn_g[i], mods, i, router_grp_w[i], router_grp_b[i], router_exp_w[i],
                    router_exp_b[i], wg_all, wu_all, wd_all,
                    final_norm_g if last else norm_mix_g[i + 1], None if last else i + 1)

    y_prompt, y_sample = x.reshape(x_prompt.shape), h.reshape(x_sample.shape)
    nbp = x_prompt.shape[0]
    kv_shape = (nbp, cfg.prompt_len, cfg.heads, cfg.head_dim)
    new_k = jnp.stack([a.reshape(kv_shape) for a in new_k], axis=1)
    new_v = jnp.stack([a.reshape(kv_shape) for a in new_v], axis=1)
    return (y_prompt, y_sample, new_k, new_v)


def kernel(x_prompt, x_sample, cache_k, cache_v, c, c_ctx, w_ada, b_ada, norm_mix_g, norm_ffn_g,
           fourier_w_in, fourier_w_out, na_w_qkv, na_w_out, na_rpb,
           router_grp_w, router_grp_b, router_exp_w, router_exp_b,
           expert_w_gate, expert_w_up, expert_w_down, final_norm_g):
    bp, lp, d = x_prompt.shape
    bl, ll, _ = x_sample.shape
    heads = na_rpb.shape[1]
    cfg = Cfg(d=d, n_prompt=bp * lp, prompt_len=lp, n_latent=bl * ll, latent_len=ll, heads=heads,
              head_dim=d // heads, past=cache_k.shape[2], ff=expert_w_gate.shape[-1], moe_block=256)
    return _forward(cfg, x_prompt, x_sample, cache_k, cache_v, c, c_ctx, w_ada, b_ada, norm_mix_g,
                    norm_ffn_g, fourier_w_in, fourier_w_out, na_w_qkv, na_w_out, na_rpb,
                    router_grp_w, router_grp_b, router_exp_w, router_exp_b,
                    expert_w_gate, expert_w_up, expert_w_down, final_norm_g)
```

```python
import collections
import functools

import jax
import jax.numpy as jnp
from jax import lax
from jax.experimental import pallas as pl
from jax.experimental.pallas import tpu as pltpu

BF = jnp.bfloat16
F32 = jnp.float32
I32 = jnp.int32

NORM_EPS = 1e-6
MASKED = -1e30
GRID_W = 64
NA_ROWS = 8
NA_COLS = 16
FOURIER_GROUPS = 8
EXPERTS_PER_GROUP = 8
N_EXPERT_GROUPS = 4
N_EXPERTS = N_EXPERT_GROUPS * EXPERTS_PER_GROUP
N_MODS = 6
COND_ROWS = 8
LANES = 128
SUBLANES = 8
SLAB_PAD = 8
ROUTE_LANES = LANES
V7X_VMEM_LIMIT = 56 * 2**20

Cfg = collections.namedtuple(
    "Cfg", "d n_prompt prompt_len n_latent latent_len heads head_dim past ff moe_block")


def _params(sem, vmem=V7X_VMEM_LIMIT):
    return pltpu.CompilerParams(dimension_semantics=sem, vmem_limit_bytes=vmem)


def _silu(x):
    return x / (1.0 + jnp.exp(-x))


def _group_of_tile(i, tm, cfg):
    start = i * tm
    return jnp.where(start < cfg.n_prompt, 0, 1 + (start - cfg.n_prompt) // cfg.latent_len)


def _mod_spec(cfg, layer, which, tm, width=None, col=False):
    width = cfg.d if width is None else width
    base = layer * COND_ROWS * N_MODS + which
    if col:
        return pl.BlockSpec((1, 1, width), lambda i, j: (base + _group_of_tile(i, tm, cfg) * N_MODS, 0, j))
    return pl.BlockSpec((1, 1, width), lambda i: (base + _group_of_tile(i, tm, cfg) * N_MODS, 0, 0))


def _ada_kernel(c_ref, w_ref, b_ref, o_ref):
    s = _silu(c_ref[...]).astype(BF)
    o_ref[0] = jnp.dot(s, w_ref[0].astype(BF), preferred_element_type=F32) + b_ref[0]


def _ada(cond, w_ada, b_ada):
    depth, d, n = w_ada.shape
    tn = min(512, n)
    return pl.pallas_call(
        _ada_kernel,
        grid=(depth, n // tn),
        in_specs=[pl.BlockSpec((COND_ROWS, d), lambda l, j: (0, 0)),
                  pl.BlockSpec((1, d, tn), lambda l, j: (l, 0, j)),
                  pl.BlockSpec((1, 1, tn), lambda l, j: (l, 0, j))],
        out_specs=pl.BlockSpec((1, COND_ROWS, tn), lambda l, j: (l, 0, j)),
        out_shape=jax.ShapeDtypeStruct((depth, COND_ROWS, n), F32),
        compiler_params=_params(("arbitrary", "arbitrary")),
        name="ada",
    )(cond, w_ada, b_ada.reshape(depth, 1, n))


def _norm_mod(x, g, sc, sh):
    y = x * lax.rsqrt(jnp.mean(x * x, axis=-1, keepdims=True) + NORM_EPS) * g
    return y * (1.0 + sc) + sh


def _split_specs(cfg, bm, width, two_d):
    pt = cfg.n_prompt // bm
    if two_d:
        return pt, [pl.BlockSpec((bm, width), lambda i, j: (jnp.minimum(i, pt - 1), j)),
                    pl.BlockSpec((bm, width), lambda i, j: (jnp.maximum(i - pt, 0), j))]
    return pt, [pl.BlockSpec((bm, width), lambda i: (jnp.minimum(i, pt - 1), 0)),
                pl.BlockSpec((bm, width), lambda i: (jnp.maximum(i - pt, 0), 0))]


def _nm_kernel(xp_ref, xl_ref, g_ref, sc_ref, sh_ref, o_ref, *, prompt_tiles):
    i = pl.program_id(0)

    @pl.when(i < prompt_tiles)
    def _prompt():
        o_ref[...] = _norm_mod(xp_ref[...], g_ref[...], sc_ref[0], sh_ref[0]).astype(o_ref.dtype)

    @pl.when(i >= prompt_tiles)
    def _latent():
        o_ref[...] = _norm_mod(xl_ref[...], g_ref[...], sc_ref[0], sh_ref[0]).astype(o_ref.dtype)


def _nm(cfg, xp, xl, g, mods, layer, sc_k, sh_k):
    d = cfg.d
    t = cfg.n_prompt + cfg.n_latent
    tm = 512
    pt, x_specs = _split_specs(cfg, tm, d, False)
    return pl.pallas_call(
        functools.partial(_nm_kernel, prompt_tiles=pt),
        grid=(t // tm,),
        in_specs=x_specs + [pl.BlockSpec((1, d), lambda i: (0, 0)),
                            _mod_spec(cfg, layer, sc_k, tm),
                            _mod_spec(cfg, layer, sh_k, tm)],
        out_specs=pl.BlockSpec((tm, d), lambda i: (i, 0)),
        out_shape=jax.ShapeDtypeStruct((t, d), BF),
        compiler_params=_params(("arbitrary",)),
        name="norm_mod",
    )(xp, xl, g.reshape(1, d), mods, mods)


def _mm_kernel(x_ref, w_ref, o_ref, *, n_scaled, scale):
    acc = jnp.dot(x_ref[...], w_ref[...], preferred_element_type=F32)
    if n_scaled:
        acc = acc * jnp.where(pl.program_id(1) < n_scaled, scale, 1.0)
    o_ref[...] = acc.astype(o_ref.dtype)


def _mm(x, w, *, row0, rows, col0, cols, out_dtype, scaled_cols=0, scale=1.0):
    k = x.shape[1]
    bm, bn = min(1024, rows), min(512, cols)
    r0, c0 = row0 // bm, col0 // bn
    return pl.pallas_call(
        functools.partial(_mm_kernel, n_scaled=scaled_cols // bn, scale=scale),
        grid=(rows // bm, cols // bn),
        in_specs=[pl.BlockSpec((bm, k), lambda i, j: (r0 + i, 0)),
                  pl.BlockSpec((k, bn), lambda i, j: (0, c0 + j))],
        out_specs=pl.BlockSpec((bm, bn), lambda i, j: (i, j)),
        out_shape=jax.ShapeDtypeStruct((rows, cols), out_dtype),
        compiler_params=_params(("arbitrary", "arbitrary")),
        name="matmul",
    )(x, w)


def _mr_kernel(x_ref, w_ref, r_ref, g_ref, o_ref):
    acc = jnp.dot(x_ref[...], w_ref[...], preferred_element_type=F32)
    o_ref[...] = r_ref[...] + g_ref[0] * acc


def _mr_split_kernel(x_ref, w_ref, rp_ref, rl_ref, g_ref, o_ref, *, prompt_tiles):
    acc = g_ref[0] * jnp.dot(x_ref[...], w_ref[...], preferred_element_type=F32)
    i = pl.program_id(0)

    @pl.when(i < prompt_tiles)
    def _prompt():
        o_ref[...] = rp_ref[...] + acc

    @pl.when(i >= prompt_tiles)
    def _latent():
        o_ref[...] = rl_ref[...] + acc


def _mm_residual(cfg, x, w, res, mods, layer, gate_k):
    t, k = x.shape
    n = w.shape[1]
    bm, bn = 1024, min(512, n)
    common = dict(
        grid=(t // bm, n // bn),
        out_specs=pl.BlockSpec((bm, bn), lambda i, j: (i, j)),
        out_shape=jax.ShapeDtypeStruct((t, n), F32),
        compiler_params=_params(("arbitrary", "arbitrary")),
        name="matmul_residual",
    )
    xw_specs = [pl.BlockSpec((bm, k), lambda i, j: (i, 0)), pl.BlockSpec((k, bn), lambda i, j: (0, j))]
    gate_spec = _mod_spec(cfg, layer, gate_k, bm, width=bn, col=True)
    if isinstance(res, tuple):
        pt, r_specs = _split_specs(cfg, bm, bn, True)
        return pl.pallas_call(
            functools.partial(_mr_split_kernel, prompt_tiles=pt),
            in_specs=xw_specs + r_specs + [gate_spec], **common,
        )(x, w, res[0], res[1], mods)
    return pl.pallas_call(
        _mr_kernel,
        in_specs=xw_specs + [pl.BlockSpec((bm, bn), lambda i, j: (i, j)), gate_spec],
        input_output_aliases={2: 0}, **common,
    )(x, w, res, mods)


def _fa_kernel(x_ref, w_ref, cs_ref, o_ref):
    u = jnp.dot(x_ref[...], w_ref[...], preferred_element_type=F32).astype(BF)
    v = jnp.dot(u, cs_ref[...], preferred_element_type=F32)
    n = u.shape[1]
    o_ref[0] = v[:, :n].astype(BF)
    o_ref[1] = v[:, n:].astype(BF)


def _fourier_in(h, w_in, cs):
    t, d = h.shape
    gd = d // FOURIER_GROUPS
    bm = 512
    return pl.pallas_call(
        _fa_kernel,
        grid=(t // bm, FOURIER_GROUPS),
        in_specs=[pl.BlockSpec((bm, d), lambda i, j: (i, 0)),
                  pl.BlockSpec((d, gd), lambda i, j: (0, j)),
                  pl.BlockSpec((gd, 2 * gd), lambda i, j: (0, 0))],
        out_specs=pl.BlockSpec((2, bm, gd), lambda i, j: (0, i, j)),
        out_shape=jax.ShapeDtypeStruct((2, t, d), BF),
        compiler_params=_params(("arbitrary", "arbitrary")),
        name="fourier_in",
    )(h, w_in, cs)


def _fb_kernel(ac_ref, as_ref, vc_ref, vs_ref, prev_ref, o_ref):
    del prev_ref
    acc = jnp.dot(ac_ref[0], vc_ref[0], preferred_element_type=F32)
    acc = acc + jnp.dot(as_ref[0], vs_ref[0], preferred_element_type=F32)
    o_ref[...] = acc.astype(o_ref.dtype)


def _fourier_seq(a, v, prev, *, row0, n_seq, seq_len, bm, bn):
    _, t, d = v.shape
    rb0, ob0 = row0 // seq_len, row0 // bm
    ni = seq_len // bm
    return pl.pallas_call(
        _fb_kernel,
        grid=(n_seq, d // bn, ni),
        in_specs=[pl.BlockSpec((1, bm, seq_len), lambda b, j, i: (0, i, 0)),
                  pl.BlockSpec((1, bm, seq_len), lambda b, j, i: (1, i, 0)),
                  pl.BlockSpec((1, seq_len, bn), lambda b, j, i: (0, rb0 + b, j)),
                  pl.BlockSpec((1, seq_len, bn), lambda b, j, i: (1, rb0 + b, j)),
                  pl.BlockSpec(memory_space=pl.ANY)],
        out_specs=pl.BlockSpec((bm, bn), lambda b, j, i: (ob0 + b * ni + i, j)),
        out_shape=jax.ShapeDtypeStruct((t, d), BF),
        input_output_aliases={4: 0},
        compiler_params=_params(("arbitrary", "arbitrary", "arbitrary")),
        name="fourier_seq",
    )(a, a, v, v, prev)


def _dft_tables(n):
    j = lax.broadcasted_iota(I32, (n, n), 0)
    k = lax.broadcasted_iota(I32, (n, n), 1)
    ang = ((j * k) % n).astype(F32) * (2.0 * jnp.pi / n)
    return jnp.cos(ang), jnp.sin(ang)


def _dft_tables_big(n, r=64):
    a = lax.broadcasted_iota(I32, (r, n), 0)
    k = lax.broadcasted_iota(I32, (r, n), 1)
    ang_hi = ((a * k) % r).astype(F32) * (2.0 * jnp.pi / r)
    ang_lo = ((a * k) % n).astype(F32) * (2.0 * jnp.pi / n)
    ch, sh = jnp.cos(ang_hi)[:, None, :], jnp.sin(ang_hi)[:, None, :]
    cl, sl = jnp.cos(ang_lo)[None, :, :], jnp.sin(ang_lo)[None, :, :]
    cos = (ch * cl - sh * sl).reshape(n, n)
    sin = (sh * cl + ch * sl).reshape(n, n)
    return cos, sin


def _fourier_mix(cfg, h, w_in):
    d = cfg.d
    gd = d // FOURIER_GROUPS
    cc, sc = _dft_tables(gd)
    cs = (jnp.concatenate([cc, sc], axis=1) * gd ** -0.5).astype(BF)
    v = _fourier_in(h, w_in, cs)
    cp, sp = _dft_tables(cfg.prompt_len)
    a_p = (jnp.stack([cp, -sp]) * cfg.prompt_len ** -0.5).astype(BF)
    cl, sl = _dft_tables_big(cfg.latent_len)
    a_l = (jnp.stack([cl, -sl]) * cfg.latent_len ** -0.5).astype(BF)
    f = _fourier_seq(a_p, v, h, row0=0, n_seq=cfg.n_prompt // cfg.prompt_len,
                     seq_len=cfg.prompt_len, bm=cfg.prompt_len, bn=d)
    f = _fourier_seq(a_l, v, f, row0=cfg.n_prompt, n_seq=cfg.n_latent // cfg.latent_len,
                     seq_len=cfg.latent_len, bm=512, bn=512)
    return f


def _softmax_pv(scores, values):
    m = functools.reduce(jnp.maximum, [jnp.max(s, axis=-1, keepdims=True) for s in scores])
    ps = [jnp.exp(s - m) for s in scores]
    l = functools.reduce(jnp.add, [jnp.sum(p, axis=-1, keepdims=True) for p in ps])
    o = functools.reduce(jnp.add, [jnp.dot(p.astype(BF), v, preferred_element_type=F32)
                                   for p, v in zip(ps, values)])
    return o / l


def _qk(q, k):
    return lax.dot_general(q, k, (((1,), (1,)), ((), ())), preferred_element_type=F32)


def _pa_kernel(q_ref, k_ref, v_ref, *rest, heads, dh):
    o_ref = rest[-1]
    for h in range(heads):
        sl = slice(h * dh, (h + 1) * dh)
        s = _qk(q_ref[:, sl], k_ref[:, sl].astype(BF))
        o_ref[:, sl] = _softmax_pv([s], [v_ref[:, sl].astype(BF)]).astype(o_ref.dtype)


def _prompt_attention(cfg, q, k, v, prev):
    lp, d = cfg.prompt_len, cfg.d
    hb = min(8, cfg.heads)
    w = hb * cfg.head_dim
    spec = pl.BlockSpec((lp, w), lambda b, g: (b, g))
    return pl.pallas_call(
        functools.partial(_pa_kernel, heads=hb, dh=cfg.head_dim),
        grid=(cfg.n_prompt // lp, d // w),
        in_specs=[spec, spec, spec, pl.BlockSpec(memory_space=pl.ANY)],
        out_specs=spec,
        out_shape=jax.ShapeDtypeStruct(prev.shape, BF),
        input_output_aliases={3: 0},
        compiler_params=_params(("arbitrary", "arbitrary")),
        name="prompt_attention",
    )(q, k, v, prev)


def _rx_kernel(r_ref, o_ref):
    r = r_ref[...]
    hi = r.astype(BF)
    r1 = r - hi.astype(F32)
    mid = r1.astype(BF)
    lo = (r1 - mid.astype(F32)).astype(BF)
    n = o_ref.shape[1]
    idx = lax.broadcasted_iota(I32, (ROUTE_LANES, n), 1)
    row = lax.broadcasted_iota(I32, (ROUTE_LANES, n), 0)
    qc, kc = idx >> 7, idx & (GRID_W - 1)
    dc = jnp.clip(kc - qc + NA_COLS - 1, 0, 2 * NA_COLS - 2)
    onehot = jnp.where(row == dc, 1.0, 0.0).astype(BF)
    t = (jnp.dot(hi, onehot, preferred_element_type=F32)
         + jnp.dot(mid, onehot, preferred_element_type=F32)
         + jnp.dot(lo, onehot, preferred_element_type=F32))
    col = lax.broadcasted_iota(I32, (1, n), 1)
    qc1, kc1 = col >> 7, col & (GRID_W - 1)
    c0 = jnp.clip(qc1 - NA_COLS // 2, 0, GRID_W - NA_COLS)
    o_ref[...] = jnp.where((kc1 >= c0) & (kc1 < c0 + NA_COLS), t, MASKED)


def _expand_rpb(rpb):
    heads, nr, nc = rpb.shape
    rows = heads * nr
    tr = rows // 4
    r2 = jnp.pad(rpb.reshape(rows, nc), ((0, 0), (0, ROUTE_LANES - nc)))
    out = pl.pallas_call(
        _rx_kernel,
        grid=(rows // tr,),
        in_specs=[pl.BlockSpec((tr, ROUTE_LANES), lambda i: (i, 0))],
        out_specs=pl.BlockSpec((tr, GRID_W * 2 * GRID_W), lambda i: (i, 0)),
        out_shape=jax.ShapeDtypeStruct((rows, GRID_W * 2 * GRID_W), F32),
        compiler_params=_params(("arbitrary",)),
        name="rpb_expand",
    )(r2)
    return out.reshape(heads, nr, GRID_W, 2 * GRID_W)


NA_QROWS = 4
NA_KROWS = NA_QROWS + NA_ROWS
NA_VARIANTS = ((0, 0), (2 * NA_QROWS, NA_QROWS), (GRID_W - NA_QROWS, GRID_W - NA_KROWS))


NA_UNROLL = 4


def _na_kernel(q_ref, k_ref, v_ref, kc_ref, vc_ref, t_ref, prev_ref, o_ref, bias_ref):
    del prev_ref
    w = GRID_W

    @pl.when(pl.program_id(1) == 0)
    def _build_bias():
        left = lax.broadcasted_iota(I32, (w, 2 * w), 1) < w
        masked = jnp.full((w, 2 * w), MASKED, F32)

        def table(r, kr):
            rs = min(max(r - NA_ROWS // 2, 0), w - NA_ROWS)
            return t_ref[0, kr - r + NA_ROWS - 1] if rs <= kr < rs + NA_ROWS else masked

        for var, (r0, k0) in enumerate(NA_VARIANTS):
            for i in range(NA_QROWS):
                for j in range(0, NA_KROWS, 2):
                    pair = jnp.where(left, table(r0 + i, k0 + j), table(r0 + i, k0 + j + 1))
                    bias_ref[var, i * w:(i + 1) * w, j * w:(j + 2) * w] = pair

    kc = kc_ref[0].astype(BF)
    vc = vc_ref[0].astype(BF)
    nq = NA_QROWS * w
    nk = NA_KROWS * w
    n_blocks = w // NA_QROWS

    def block(qb, carry):
        var = jnp.where(qb == 0, 0, jnp.where(qb == n_blocks - 1, 2, 1))
        k0 = jnp.clip(qb * NA_QROWS - NA_ROWS // 2, 0, w - NA_KROWS)
        q0 = pl.multiple_of(qb * nq, nq)
        ks = pl.multiple_of(k0 * w, w)
        q = q_ref[pl.ds(q0, nq), :]
        s_lat = _qk(q, k_ref[pl.ds(ks, nk), :]) + bias_ref[var]
        s_ctx = _qk(q, kc)
        o = _softmax_pv([s_lat, s_ctx], [v_ref[pl.ds(ks, nk), :], vc])
        o_ref[pl.ds(q0, nq), :] = o.astype(o_ref.dtype)
        return carry

    def block_group(g, carry):
        for u in range(NA_UNROLL):
            block(g * NA_UNROLL + u, carry)
        return carry

    lax.fori_loop(0, n_blocks // NA_UNROLL, block_group, 0)


def _latent_attention(cfg, qkv, cache_k, cache_v, t_bias, prev):
    d, dh, heads, ls = cfg.d, cfg.head_dim, cfg.heads, cfg.latent_len
    nb = cfg.n_latent // ls
    rb0 = cfg.n_prompt // ls
    w = GRID_W
    return pl.pallas_call(
        _na_kernel,
        grid=(heads, nb),
        in_specs=[pl.BlockSpec((ls, dh), lambda h, b: (b, h)),
                  pl.BlockSpec((ls, dh), lambda h, b: (b, heads + h)),
                  pl.BlockSpec((ls, dh), lambda h, b: (b, 2 * heads + h)),
                  pl.BlockSpec((1, cfg.past, dh), lambda h, b: (b, 0, h)),
                  pl.BlockSpec((1, cfg.past, dh), lambda h, b: (b, 0, h)),
                  pl.BlockSpec((1, 2 * NA_ROWS - 1, w, 2 * w), lambda h, b: (h, 0, 0, 0)),
                  pl.BlockSpec(memory_space=pl.ANY)],
        out_specs=pl.BlockSpec((ls, dh), lambda h, b: (rb0 + b, h)),
        out_shape=jax.ShapeDtypeStruct(prev.shape, BF),
        scratch_shapes=[pltpu.VMEM((len(NA_VARIANTS), NA_QROWS * w, NA_KROWS * w), F32)],
        input_output_aliases={6: 0},
        compiler_params=_params(("arbitrary", "arbitrary")),
        name="latent_attention",
    )(qkv, qkv, qkv, cache_k, cache_v, t_bias, prev)


U32 = jnp.uint32
HIGH_HALF = 0xFFFF0000


def _slab(d):
    rows = d // (2 * LANES)
    return rows, -(-rows // SUBLANES) * SUBLANES + SLAB_PAD


def _bf16_bits(x):
    return lax.bitcast_convert_type(x.astype(BF).astype(F32), U32)


def _store_slabs(ref, bits):
    n, d = bits.shape
    rows, pitch = _slab(d)
    for r in range(rows):
        lo = bits[:, r * LANES:(r + 1) * LANES]
        hi = bits[:, d // 2 + r * LANES:d // 2 + (r + 1) * LANES]
        ref[pl.ds(r, n, stride=pitch), :] = (hi & jnp.uint32(HIGH_HALF)) | (lo >> 16)
    for r in range(rows, pitch):
        ref[pl.ds(r, n, stride=pitch), :] = jnp.zeros((n, LANES), U32)


def _load_slab_piece(ref, lead, n, pitch, r):
    w = ref[lead + (pl.ds(r, n, stride=pitch), slice(None))]
    return (lax.bitcast_convert_type(w << 16, F32),
            lax.bitcast_convert_type(w & jnp.uint32(HIGH_HALF), F32))


def _nmr_kernel(x_ref, g_ref, sc_ref, sh_ref, wr_ref, br_ref, hs_ref, ri_ref, rg_ref, cnt_ref, base_ref):
    tm, d = x_ref.shape

    @pl.when(pl.program_id(0) == 0)
    def _init():
        base_ref[...] = jnp.zeros_like(base_ref)

    h = _norm_mod(x_ref[...], g_ref[...], sc_ref[0], sh_ref[0])
    h_hi = h.astype(BF)
    h_hi32 = h_hi.astype(F32)
    _store_slabs(hs_ref, lax.bitcast_convert_type(h_hi32, U32))

    h_lo = (h - h_hi32).astype(BF)
    wr = wr_ref[...]
    w_hi = wr.astype(BF)
    w_lo = (wr - w_hi.astype(F32)).astype(BF)
    logits = (jnp.dot(h_hi, w_hi, preferred_element_type=F32)
              + jnp.dot(h_lo, w_hi, preferred_element_type=F32)
              + jnp.dot(h_hi, w_lo, preferred_element_type=F32)) + br_ref[...]

    lane = lax.broadcasted_iota(I32, (tm, ROUTE_LANES), 1)
    lane_f = lane.astype(F32)

    def first_argmax(v):
        m = jnp.max(v, axis=-1, keepdims=True)
        first = jnp.min(jnp.where(v == m, lane_f, float(ROUTE_LANES)), axis=-1, keepdims=True)
        return m, first.astype(I32)

    gl = jnp.where(lane < N_EXPERT_GROUPS, logits, MASKED)
    gmax, gi = first_argmax(gl)
    g_w = 1.0 / jnp.sum(jnp.exp(gl - gmax), axis=-1, keepdims=True)
    e0 = N_EXPERT_GROUPS + gi * EXPERTS_PER_GROUP
    el = jnp.where((lane >= e0) & (lane < e0 + EXPERTS_PER_GROUP), logits, MASKED)
    m1, i1 = first_argmax(el)
    m2, i2 = first_argmax(jnp.where(lane == i1, MASKED, el))
    tt = jnp.exp(m2 - m1)
    w1 = 1.0 / (1.0 + tt)
    eid0, eid1 = i1 - N_EXPERT_GROUPS, i2 - N_EXPERT_GROUPS

    oh = jnp.where((lane == eid0) | (lane == eid1), 1.0, 0.0)
    row = lax.broadcasted_iota(I32, (tm, tm), 0)
    colt = lax.broadcasted_iota(I32, (tm, tm), 1)
    earlier = jnp.where(row > colt, 1.0, 0.0).astype(BF)
    before = jnp.dot(earlier, oh.astype(BF), preferred_element_type=F32) + base_ref[0:1, :]
    rank0 = jnp.sum(jnp.where(lane == eid0, before, 0.0), axis=-1, keepdims=True).astype(I32)
    rank1 = jnp.sum(jnp.where(lane == eid1, before, 0.0), axis=-1, keepdims=True).astype(I32)
    base_ref[...] = base_ref[...] + jnp.sum(oh, axis=0, keepdims=True)

    ri_ref[...] = jnp.where(lane == 0, eid0, jnp.where(lane == 1, eid1,
                            jnp.where(lane == 2, rank0, jnp.where(lane == 3, rank1, 0))))
    rg_ref[...] = jnp.where(lane == 0, g_w * w1, jnp.where(lane == 1, g_w * (tt * w1), 0.0))
    cnt_ref[...] = base_ref[...]


def _norm_route(cfg, x, g, mods, layer, w_route, b_route):
    t, d = x.shape
    tm = 256
    _, pitch = _slab(d)
    return pl.pallas_call(
        _nmr_kernel,
        grid=(t // tm,),
        in_specs=[pl.BlockSpec((tm, d), lambda i: (i, 0)),
                  pl.BlockSpec((1, d), lambda i: (0, 0)),
                  _mod_spec(cfg, layer, 4, tm),
                  _mod_spec(cfg, layer, 3, tm),
                  pl.BlockSpec((d, ROUTE_LANES), lambda i: (0, 0)),
                  pl.BlockSpec((1, ROUTE_LANES), lambda i: (0, 0))],
        out_specs=[pl.BlockSpec((tm * pitch, LANES), lambda i: (i, 0)),
                   pl.BlockSpec((tm, ROUTE_LANES), lambda i: (i, 0)),
                   pl.BlockSpec((tm, ROUTE_LANES), lambda i: (i, 0)),
                   pl.BlockSpec((SUBLANES, ROUTE_LANES), lambda i: (0, 0))],
        out_shape=[jax.ShapeDtypeStruct((t * pitch, LANES), U32),
                   jax.ShapeDtypeStruct((t, ROUTE_LANES), I32),
                   jax.ShapeDtypeStruct((t, ROUTE_LANES), F32),
                   jax.ShapeDtypeStruct((SUBLANES, ROUTE_LANES), F32)],
        scratch_shapes=[pltpu.VMEM((SUBLANES, ROUTE_LANES), F32)],
        compiler_params=_params(("arbitrary",)),
        name="norm_route",
    )(x, g.reshape(1, d), mods, mods, w_route, b_route)


MOVE_TOKENS = 256


def _wait_bytes_of(buf_ref, sem):
    pltpu.make_async_copy(buf_ref, buf_ref, sem).wait()


def _ds_kernel(dest_ref, zstart_ref, hs_ref, xs_ref, zbuf_ref, sem, zsem):
    n = MOVE_TOKENS
    pitch = hs_ref.shape[0] // n
    bm = zbuf_ref.shape[0]
    n_blocks = xs_ref.shape[0] // bm

    @pl.when(pl.program_id(0) == 0)
    def _zero_fill():
        zbuf_ref[...] = jnp.zeros_like(zbuf_ref)
        n_used = zstart_ref[0, 0, N_EXPERTS]
        targets = [(zstart_ref[0, 0, e] >= 0, zstart_ref[0, 0, e]) for e in range(N_EXPERTS)]
        targets += [(n_used + j < n_blocks, (n_used + j) * bm) for j in range(N_EXPERTS)]
        for wait in (False, True):
            for live, start in targets:
                @pl.when(live)
                def _():
                    copy = pltpu.make_async_copy(zbuf_ref, xs_ref.at[pl.ds(start, bm)], zsem)
                    copy.wait() if wait else copy.start()

    def issue(tk, carry):
        src = hs_ref.at[pl.ds(pl.multiple_of(tk * pitch, SUBLANES), pitch), :]
        for k in range(2):
            pltpu.make_async_copy(src, xs_ref.at[dest_ref[0, 0, 2 * tk + k]], sem).start()
        return carry

    lax.fori_loop(0, n, issue, 0)
    _wait_bytes_of(xs_ref.at[pl.ds(0, 2 * n)], sem)


def _dispatch(hs, dest, zero_plan, n_slots, bm):
    n = MOVE_TOKENS
    t = dest.shape[0] // 2
    pitch = hs.shape[0] // t
    return pl.pallas_call(
        _ds_kernel,
        grid=(t // n,),
        in_specs=[pl.BlockSpec((1, 1, 2 * n), lambda i: (i, 0, 0), memory_space=pltpu.SMEM),
                  pl.BlockSpec((1, 1, N_EXPERTS + 1), lambda i: (0, 0, 0), memory_space=pltpu.SMEM),
                  pl.BlockSpec((n * pitch, LANES), lambda i: (i, 0))],
        out_specs=pl.BlockSpec(memory_space=pl.ANY),
        out_shape=jax.ShapeDtypeStruct((n_slots, pitch, LANES), U32),
        scratch_shapes=[pltpu.VMEM((bm, pitch, LANES), U32), pltpu.SemaphoreType.DMA, pltpu.SemaphoreType.DMA],
        compiler_params=_params(("arbitrary",)),
        name="moe_dispatch",
    )(dest.reshape(t // n, 1, 2 * n), zero_plan.reshape(1, 1, N_EXPERTS + 1), hs)


def _ex_kernel(blk_e_ref, n_used_ref, x_ref, wg_ref, wu_ref, wd_ref, o_ref, xb_ref):
    del blk_e_ref
    bm, d = xb_ref.shape
    rows, pitch = _slab(d)

    @pl.when(pl.program_id(0) < n_used_ref[0])
    def _compute():
        for r in range(rows):
            lo, hi = _load_slab_piece(x_ref, (), bm, pitch, r)
            xb_ref[:, r * LANES:(r + 1) * LANES] = lo.astype(BF)
            xb_ref[:, d // 2 + r * LANES:d // 2 + (r + 1) * LANES] = hi.astype(BF)
        x = xb_ref[...]
        g = jnp.dot(x, wg_ref[0], preferred_element_type=F32)
        u = jnp.dot(x, wu_ref[0], preferred_element_type=F32)
        y = jnp.dot((_silu(g) * u).astype(BF), wd_ref[0], preferred_element_type=F32)
        _store_slabs(o_ref, _bf16_bits(y))


def _experts(cfg, xs, blk_e, n_used, wg, wu, wd):
    n_slots, sr, sw = xs.shape
    d, ff, bm = cfg.d, cfg.ff, cfg.moe_block
    nb = n_slots // bm

    def blk(b, be, nu):
        return jnp.minimum(b, nu[0] - 1)

    grid_spec = pltpu.PrefetchScalarGridSpec(
        num_scalar_prefetch=2,
        grid=(nb,),
        in_specs=[pl.BlockSpec((bm * sr, sw), lambda b, be, nu: (blk(b, be, nu), 0)),
                  pl.BlockSpec((1, d, ff), lambda b, be, nu: (be[blk(b, be, nu)], 0, 0)),
                  pl.BlockSpec((1, d, ff), lambda b, be, nu: (be[blk(b, be, nu)], 0, 0)),
                  pl.BlockSpec((1, ff, d), lambda b, be, nu: (be[blk(b, be, nu)], 0, 0))],
        out_specs=pl.BlockSpec((bm * sr, sw), lambda b, be, nu: (blk(b, be, nu), 0)),
        scratch_shapes=[pltpu.VMEM((bm, d), BF)],
    )
    ys = pl.pallas_call(
        _ex_kernel,
        grid_spec=grid_spec,
        out_shape=jax.ShapeDtypeStruct((n_slots * sr, sw), U32),
        input_output_aliases={2: 0},
        compiler_params=_params(("arbitrary",)),
        name="moe_experts",
    )(blk_e, n_used, xs.reshape(n_slots * sr, sw), wg, wu, wd)
    return ys.reshape(n_slots, sr, sw)


def _gather_combine(dcur_ref, dnext_ref, x_ref, ys_ref, rg_ref, gt_ref, o_ref, ybuf_ref, sems):
    tm, d = x_ref.shape
    rows, pitch = _slab(d)
    copy_rows = pitch - SLAB_PAD
    i, n_steps = pl.program_id(0), pl.num_programs(0)

    def start_gather(dref, slot):
        def issue(tk, carry):
            for k in range(2):
                row0 = pl.multiple_of((k * tm + tk) * pitch, SUBLANES)
                pltpu.make_async_copy(ys_ref.at[dref[0, 0, 2 * tk + k], pl.ds(0, copy_rows), :],
                                      ybuf_ref.at[slot, pl.ds(row0, copy_rows), :], sems.at[slot]).start()
            return carry
        lax.fori_loop(0, tm, issue, 0)

    @pl.when(i == 0)
    def _first():
        start_gather(dcur_ref, 0)

    @pl.when(i + 1 < n_steps)
    def _next():
        start_gather(dnext_ref, (i + 1) % 2)

    slot = i % 2
    _wait_bytes_of(ybuf_ref.at[slot, pl.ds(0, 2 * tm * copy_rows), :], sems.at[slot])
    g0, g1 = rg_ref[:, 0:1], rg_ref[:, 1:2]
    for r in range(rows):
        lo0, hi0 = _load_slab_piece(ybuf_ref, (slot,), tm, pitch, r)
        lo1, hi1 = _load_slab_piece(ybuf_ref, (slot,), tm, pitch, tm * pitch + r)
        for c0, y0, y1 in ((r * LANES, lo0, lo1), (d // 2 + r * LANES, hi0, hi1)):
            sl = slice(c0, c0 + LANES)
            o_ref[:, sl] = x_ref[:, sl] + gt_ref[0][:, sl] * (y0 * g0 + y1 * g1)


def _cb_next_kernel(dcur_ref, dnext_ref, x_ref, ys_ref, rg_ref, gt_ref, g_ref, sc_ref, sh_ref,
                    o_ref, hn_ref, ybuf_ref, sems):
    _gather_combine(dcur_ref, dnext_ref, x_ref, ys_ref, rg_ref, gt_ref, o_ref, ybuf_ref, sems)
    hn_ref[...] = _norm_mod(o_ref[...], g_ref[...], sc_ref[0], sh_ref[0]).astype(hn_ref.dtype)


def _cb_final_kernel(dcur_ref, dnext_ref, x_ref, ys_ref, rg_ref, gt_ref, g_ref,
                     yp_ref, yl_ref, ybuf_ref, sems, xn_ref, *, prompt_tiles):
    _gather_combine(dcur_ref, dnext_ref, x_ref, ys_ref, rg_ref, gt_ref, xn_ref, ybuf_ref, sems)
    xn = xn_ref[...]
    y = xn * lax.rsqrt(jnp.mean(xn * xn, axis=-1, keepdims=True) + NORM_EPS) * g_ref[...]
    i = pl.program_id(0)

    @pl.when(i < prompt_tiles)
    def _prompt():
        yp_ref[...] = y

    @pl.when(i >= prompt_tiles)
    def _latent():
        yl_ref[...] = y


def _combine(cfg, x, ys, dest, rg, mods, layer, g, next_layer):
    t, d = x.shape
    tm = MOVE_TOKENS
    _, pitch = _slab(d)
    n = t // tm
    dest3 = dest.reshape(n, 1, 2 * tm)
    row = pl.BlockSpec((tm, d), lambda i: (i, 0))
    in_specs = [pl.BlockSpec((1, 1, 2 * tm), lambda i: (i, 0, 0), memory_space=pltpu.SMEM),
                pl.BlockSpec((1, 1, 2 * tm), lambda i: (jnp.minimum(i + 1, n - 1), 0, 0),
                             memory_space=pltpu.SMEM),
                row,
                pl.BlockSpec(memory_space=pl.ANY),
                pl.BlockSpec((tm, ROUTE_LANES), lambda i: (i, 0)),
                _mod_spec(cfg, layer, 5, tm),
                pl.BlockSpec((1, d), lambda i: (0, 0))]
    scratch = [pltpu.VMEM((2, 2 * tm * pitch, LANES), U32), pltpu.SemaphoreType.DMA((2,))]
    args = [dest3, dest3, x, ys, rg, mods, g.reshape(1, d)]
    if next_layer is not None:
        return pl.pallas_call(
            _cb_next_kernel,
            grid=(n,),
            in_specs=in_specs + [_mod_spec(cfg, next_layer, 1, tm), _mod_spec(cfg, next_layer, 0, tm)],
            out_specs=[row, row],
            out_shape=[jax.ShapeDtypeStruct((t, d), F32), jax.ShapeDtypeStruct((t, d), BF)],
            scratch_shapes=scratch,
            input_output_aliases={2: 0},
            compiler_params=_params(("arbitrary",)),
            name="moe_combine_norm",
        )(*args, mods, mods)
    pt = cfg.n_prompt // tm
    return pl.pallas_call(
        functools.partial(_cb_final_kernel, prompt_tiles=pt),
        grid=(n,),
        in_specs=in_specs,
        out_specs=[pl.BlockSpec((tm, d), lambda i: (jnp.minimum(i, pt - 1), 0)),
                   pl.BlockSpec((tm, d), lambda i: (jnp.maximum(i - pt, 0), 0))],
        out_shape=[jax.ShapeDtypeStruct((cfg.n_prompt, d), F32), jax.ShapeDtypeStruct((cfg.n_latent, d), F32)],
        scratch_shapes=scratch + [pltpu.VMEM((tm, d), F32)],
        compiler_params=_params(("arbitrary",)),
        name="moe_combine_final",
    )(*args)


def _moe(cfg, x, g, mods, layer, w_grp, b_grp, w_exp, b_exp, wg, wu, wd, g_after, next_layer):
    t, d = x.shape
    bm = cfg.moe_block
    pad = ROUTE_LANES - N_EXPERT_GROUPS - N_EXPERTS
    w_route = jnp.pad(jnp.concatenate([w_grp, w_exp], axis=1), ((0, 0), (0, pad)))
    b_route = jnp.pad(jnp.concatenate([b_grp, b_exp]), (0, pad)).reshape(1, ROUTE_LANES)
    hs, ri, rg, cnt = _norm_route(cfg, x, g, mods, layer, w_route, b_route)

    counts = cnt[0, :N_EXPERTS].astype(I32)
    padded = (counts + bm - 1) // bm * bm
    ends = jnp.cumsum(padded)
    starts = ends - padded
    eid, rank = ri[:, 0:2], ri[:, 2:4]
    start_of = jnp.sum(jnp.where(eid[:, :, None] == jnp.arange(N_EXPERTS, dtype=I32), starts, 0), axis=-1)
    dest = (start_of + rank).reshape(-1)
    n_blocks = -(-2 * t // bm) + N_EXPERTS
    blk_e = jnp.minimum(jnp.sum(ends[None, :] <= (jnp.arange(n_blocks, dtype=I32) * bm)[:, None], axis=1),
                        N_EXPERTS - 1).astype(I32) + layer * N_EXPERTS
    n_used = (ends[-1:] // bm).astype(I32)
    zero_plan = jnp.concatenate([jnp.where(counts > 0, ends - bm, -1).astype(I32), n_used])

    xs = _dispatch(hs, dest, zero_plan, n_blocks * bm, bm)
    ys = _experts(cfg, xs, blk_e, n_used, wg, wu, wd)
    return _combine(cfg, x, ys, dest, rg, mods, layer, g_after, next_layer)


def _forward(cfg, x_prompt, x_sample, cache_k, cache_v, c, c_ctx, w_ada, b_ada, norm_mix_g, norm_ffn_g,
             fourier_w_in, fourier_w_out, na_w_qkv, na_w_out, na_rpb,
             router_grp_w, router_grp_b, router_exp_w, router_exp_b,
             expert_w_gate, expert_w_up, expert_w_down, final_norm_g):
    d = cfg.d
    depth = w_ada.shape[0]
    xp, xl = x_prompt.reshape(cfg.n_prompt, d), x_sample.reshape(cfg.n_latent, d)

    nb_lat = cfg.n_latent // cfg.latent_len
    cond = jnp.concatenate([c_ctx[None], c, jnp.zeros((COND_ROWS - 1 - nb_lat, d), F32)], axis=0)
    mods = _ada(cond, w_ada, b_ada).reshape(depth * COND_ROWS * N_MODS, 1, d)

    n_all = depth * N_EXPERTS
    wg_all = expert_w_gate.astype(BF).reshape(n_all, d, cfg.ff)
    wu_all = expert_w_up.astype(BF).reshape(n_all, d, cfg.ff)
    wd_all = expert_w_down.astype(BF).reshape(n_all, cfg.ff, d)

    x = (xp, xl)
    h = _nm(cfg, xp, xl, norm_mix_g[0], mods, 0, 1, 0)
    new_k, new_v = [], []
    for i in range(depth):
        j = i // 2
        if i % 2 == 0:
            f = _fourier_mix(cfg, h, fourier_w_in[j].astype(BF))
            x = _mm_residual(cfg, f, fourier_w_out[j].astype(BF), x, mods, i, 2)
        else:
            w_qkv = na_w_qkv[j].astype(BF)
            scale = cfg.head_dim ** -0.5
            q_p = _mm(h, w_qkv, row0=0, rows=cfg.n_prompt, col0=0, cols=d, out_dtype=BF,
                      scaled_cols=d, scale=scale)
            k_p = _mm(h, w_qkv, row0=0, rows=cfg.n_prompt, col0=d, cols=d, out_dtype=F32)
            v_p = _mm(h, w_qkv, row0=0, rows=cfg.n_prompt, col0=2 * d, cols=d, out_dtype=F32)
            qkv_l = _mm(h, w_qkv, row0=cfg.n_prompt, rows=cfg.n_latent, col0=0, cols=3 * d, out_dtype=BF,
                        scaled_cols=d, scale=scale)
            new_k.append(k_p)
            new_v.append(v_p)
            o = _prompt_attention(cfg, q_p, k_p, v_p, h)
            t_bias = _expand_rpb(na_rpb[j])
            ck = cache_k[:, j].reshape(nb_lat, cfg.past, d)
            cv = cache_v[:, j].reshape(nb_lat, cfg.past, d)
            o = _latent_attention(cfg, qkv_l, ck, cv, t_bias, o)
            x = _mm_residual(cfg, o, na_w_out[j].astype(BF), x, mods, i, 2)
        last = i == depth - 1
        x, h = _moe(cfg, x, norm_ffn_g[i], mods, i, router_grp_w[i], router_grp_b[i], router_exp_w[i],
                    router_exp_b[i], wg_all, wu_all, wd_all,
                    final_norm_g if last else norm_mix_g[i + 1], None if last else i + 1)

    y_prompt, y_sample = x.reshape(x_prompt.shape), h.reshape(x_sample.shape)
    nbp = x_prompt.shape[0]
    kv_shape = (nbp, cfg.prompt_len, cfg.heads, cfg.head_dim)
    new_k = jnp.stack([a.reshape(kv_shape) for a in new_k], axis=1)
    new_v = jnp.stack([a.reshape(kv_shape) for a in new_v], axis=1)
    return (y_prompt, y_sample, new_k, new_v)


def kernel(x_prompt, x_sample, cache_k, cache_v, c, c_ctx, w_ada, b_ada, norm_mix_g, norm_ffn_g,
           fourier_w_in, fourier_w_out, na_w_qkv, na_w_out, na_rpb,
           router_grp_w, router_grp_b, router_exp_w, router_exp_b,
           expert_w_gate, expert_w_up, expert_w_down, final_norm_g):
    bp, lp, d = x_prompt.shape
    bl, ll, _ = x_sample.shape
    heads = na_rpb.shape[1]
    cfg = Cfg(d=d, n_prompt=bp * lp, prompt_len=lp, n_latent=bl * ll, latent_len=ll, heads=heads,
              head_dim=d // heads, past=cache_k.shape[2], ff=expert_w_gate.shape[-1], moe_block=256)
    return _forward(cfg, x_prompt, x_sample, cache_k, cache_v, c, c_ctx, w_ada, b_ada, norm_mix_g,
                    norm_ffn_g, fourier_w_in, fourier_w_out, na_w_qkv, na_w_out, na_rpb,
                    router_grp_w, router_grp_b, router_exp_w, router_exp_b,
                    expert_w_gate, expert_w_up, expert_w_down, final_norm_g)
```

```python
import collections
import functools

import jax
import jax.numpy as jnp
from jax import lax
from jax.experimental import pallas as pl
from jax.experimental.pallas import tpu as pltpu

BF = jnp.bfloat16
F32 = jnp.float32
I32 = jnp.int32

NORM_EPS = 1e-6
MASKED = -1e30
GRID_W = 64
NA_ROWS = 8
NA_COLS = 16
FOURIER_GROUPS = 8
EXPERTS_PER_GROUP = 8
N_EXPERT_GROUPS = 4
N_EXPERTS = N_EXPERT_GROUPS * EXPERTS_PER_GROUP
N_MODS = 6
COND_ROWS = 8
LANES = 128
SUBLANES = 8
SLAB_PAD = 8
ROUTE_LANES = LANES
V7X_VMEM_LIMIT = 56 * 2**20

Cfg = collections.namedtuple(
    "Cfg", "d n_prompt prompt_len n_latent latent_len heads head_dim past ff moe_block")


def _params(sem, vmem=V7X_VMEM_LIMIT):
    return pltpu.CompilerParams(dimension_semantics=sem, vmem_limit_bytes=vmem)


def _silu(x):
    return x / (1.0 + jnp.exp(-x))


def _group_of_tile(i, tm, cfg):
    start = i * tm
    return jnp.where(start < cfg.n_prompt, 0, 1 + (start - cfg.n_prompt) // cfg.latent_len)


def _mod_spec(cfg, layer, which, tm, width=None, col=False):
    width = cfg.d if width is None else width
    base = layer * COND_ROWS * N_MODS + which
    if col:
        return pl.BlockSpec((1, 1, width), lambda i, j: (base + _group_of_tile(i, tm, cfg) * N_MODS, 0, j))
    return pl.BlockSpec((1, 1, width), lambda i: (base + _group_of_tile(i, tm, cfg) * N_MODS, 0, 0))


def _ada_kernel(c_ref, w_ref, b_ref, o_ref):
    s = _silu(c_ref[...]).astype(BF)
    o_ref[0] = jnp.dot(s, w_ref[0].astype(BF), preferred_element_type=F32) + b_ref[0]


def _ada(cond, w_ada, b_ada):
    depth, d, n = w_ada.shape
    tn = min(512, n)
    return pl.pallas_call(
        _ada_kernel,
        grid=(depth, n // tn),
        in_specs=[pl.BlockSpec((COND_ROWS, d), lambda l, j: (0, 0)),
                  pl.BlockSpec((1, d, tn), lambda l, j: (l, 0, j)),
                  pl.BlockSpec((1, 1, tn), lambda l, j: (l, 0, j))],
        out_specs=pl.BlockSpec((1, COND_ROWS, tn), lambda l, j: (l, 0, j)),
        out_shape=jax.ShapeDtypeStruct((depth, COND_ROWS, n), F32),
        compiler_params=_params(("arbitrary", "arbitrary")),
        name="ada",
    )(cond, w_ada, b_ada.reshape(depth, 1, n))


def _norm_mod(x, g, sc, sh):
    y = x * lax.rsqrt(jnp.mean(x * x, axis=-1, keepdims=True) + NORM_EPS) * g
    return y * (1.0 + sc) + sh


def _split_specs(cfg, bm, width, two_d):
    pt = cfg.n_prompt // bm
    if two_d:
        return pt, [pl.BlockSpec((bm, width), lambda i, j: (jnp.minimum(i, pt - 1), j)),
                    pl.BlockSpec((bm, width), lambda i, j: (jnp.maximum(i - pt, 0), j))]
    return pt, [pl.BlockSpec((bm, width), lambda i: (jnp.minimum(i, pt - 1), 0)),
                pl.BlockSpec((bm, width), lambda i: (jnp.maximum(i - pt, 0), 0))]


def _nm_kernel(xp_ref, xl_ref, g_ref, sc_ref, sh_ref, o_ref, *, prompt_tiles):
    i = pl.program_id(0)

    @pl.when(i < prompt_tiles)
    def _prompt():
        o_ref[...] = _norm_mod(xp_ref[...], g_ref[...], sc_ref[0], sh_ref[0]).astype(o_ref.dtype)

    @pl.when(i >= prompt_tiles)
    def _latent():
        o_ref[...] = _norm_mod(xl_ref[...], g_ref[...], sc_ref[0], sh_ref[0]).astype(o_ref.dtype)


def _nm(cfg, xp, xl, g, mods, layer, sc_k, sh_k):
    d = cfg.d
    t = cfg.n_prompt + cfg.n_latent
    tm = 512
    pt, x_specs = _split_specs(cfg, tm, d, False)
    return pl.pallas_call(
        functools.partial(_nm_kernel, prompt_tiles=pt),
        grid=(t // tm,),
        in_specs=x_specs + [pl.BlockSpec((1, d), lambda i: (0, 0)),
                            _mod_spec(cfg, layer, sc_k, tm),
                            _mod_spec(cfg, layer, sh_k, tm)],
        out_specs=pl.BlockSpec((tm, d), lambda i: (i, 0)),
        out_shape=jax.ShapeDtypeStruct((t, d), BF),
        compiler_params=_params(("arbitrary",)),
        name="norm_mod",
    )(xp, xl, g.reshape(1, d), mods, mods)


def _mm_kernel(x_ref, w_ref, o_ref, *, n_scaled, scale):
    acc = jnp.dot(x_ref[...], w_ref[...], preferred_element_type=F32)
    if n_scaled:
        acc = acc * jnp.where(pl.program_id(1) < n_scaled, scale, 1.0)
    o_ref[...] = acc.astype(o_ref.dtype)


def _mm(x, w, *, row0, rows, col0, cols, out_dtype, scaled_cols=0, scale=1.0):
    k = x.shape[1]
    bm, bn = min(1024, rows), min(512, cols)
    r0, c0 = row0 // bm, col0 // bn
    return pl.pallas_call(
        functools.partial(_mm_kernel, n_scaled=scaled_cols // bn, scale=scale),
        grid=(rows // bm, cols // bn),
        in_specs=[pl.BlockSpec((bm, k), lambda i, j: (r0 + i, 0)),
                  pl.BlockSpec((k, bn), lambda i, j: (0, c0 + j))],
        out_specs=pl.BlockSpec((bm, bn), lambda i, j: (i, j)),
        out_shape=jax.ShapeDtypeStruct((rows, cols), out_dtype),
        compiler_params=_params(("arbitrary", "arbitrary")),
        name="matmul",
    )(x, w)


def _mr_kernel(x_ref, w_ref, r_ref, g_ref, o_ref):
    acc = jnp.dot(x_ref[...], w_ref[...], preferred_element_type=F32)
    o_ref[...] = r_ref[...] + g_ref[0] * acc


def _mr_split_kernel(x_ref, w_ref, rp_ref, rl_ref, g_ref, o_ref, *, prompt_tiles):
    acc = g_ref[0] * jnp.dot(x_ref[...], w_ref[...], preferred_element_type=F32)
    i = pl.program_id(0)

    @pl.when(i < prompt_tiles)
    def _prompt():
        o_ref[...] = rp_ref[...] + acc

    @pl.when(i >= prompt_tiles)
    def _latent():
        o_ref[...] = rl_ref[...] + acc


def _mm_residual(cfg, x, w, res, mods, layer, gate_k):
    t, k = x.shape
    n = w.shape[1]
    bm, bn = 1024, min(512, n)
    common = dict(
        grid=(t // bm, n // bn),
        out_specs=pl.BlockSpec((bm, bn), lambda i, j: (i, j)),
        out_shape=jax.ShapeDtypeStruct((t, n), F32),
        compiler_params=_params(("arbitrary", "arbitrary")),
        name="matmul_residual",
    )
    xw_specs = [pl.BlockSpec((bm, k), lambda i, j: (i, 0)), pl.BlockSpec((k, bn), lambda i, j: (0, j))]
    gate_spec = _mod_spec(cfg, layer, gate_k, bm, width=bn, col=True)
    if isinstance(res, tuple):
        pt, r_specs = _split_specs(cfg, bm, bn, True)
        return pl.pallas_call(
            functools.partial(_mr_split_kernel, prompt_tiles=pt),
            in_specs=xw_specs + r_specs + [gate_spec], **common,
        )(x, w, res[0], res[1], mods)
    return pl.pallas_call(
        _mr_kernel,
        in_specs=xw_specs + [pl.BlockSpec((bm, bn), lambda i, j: (i, j)), gate_spec],
        input_output_aliases={2: 0}, **common,
    )(x, w, res, mods)


def _fa_kernel(x_ref, w_ref, cs_ref, o_ref):
    u = jnp.dot(x_ref[...], w_ref[...], preferred_element_type=F32).astype(BF)
    v = jnp.dot(u, cs_ref[...], preferred_element_type=F32)
    n = u.shape[1]
    o_ref[...] = _pack_pair(_bf16_bits(v[:, :n]), _bf16_bits(v[:, n:]))


def _fourier_in(h, w_in, cs):
    t, d = h.shape
    gd = d // FOURIER_GROUPS
    bm = 512
    return pl.pallas_call(
        _fa_kernel,
        grid=(t // bm, FOURIER_GROUPS),
        in_specs=[pl.BlockSpec((bm, d), lambda i, j: (i, 0)),
                  pl.BlockSpec((d, gd), lambda i, j: (0, j)),
                  pl.BlockSpec((gd, 2 * gd), lambda i, j: (0, 0))],
        out_specs=pl.BlockSpec((bm, gd), lambda i, j: (i, j)),
        out_shape=jax.ShapeDtypeStruct((t, d), U32),
        compiler_params=_params(("arbitrary", "arbitrary")),
        name="fourier_in",
    )(h, w_in, cs)


def _fb_kernel(a_ref, v_ref, prev_ref, o_ref):
    del prev_ref
    vc, vs = _unpack_pair(v_ref[...])
    acc = jnp.dot(a_ref[0], vc.astype(BF), preferred_element_type=F32)
    acc = acc + jnp.dot(a_ref[1], vs.astype(BF), preferred_element_type=F32)
    o_ref[...] = acc.astype(o_ref.dtype)


def _fourier_seq_dense(a, v, prev, *, n_seq, seq_len):
    t, d = v.shape
    return pl.pallas_call(
        _fb_kernel,
        grid=(n_seq,),
        in_specs=[pl.BlockSpec((2, seq_len, seq_len), lambda b: (0, 0, 0)),
                  pl.BlockSpec((seq_len, d), lambda b: (b, 0)),
                  pl.BlockSpec(memory_space=pl.ANY)],
        out_specs=pl.BlockSpec((seq_len, d), lambda b: (b, 0)),
        out_shape=jax.ShapeDtypeStruct((t, d), BF),
        input_output_aliases={2: 0},
        compiler_params=_params(("arbitrary",)),
        name="fourier_seq_dense",
    )(a, v, prev)


FFT_RADIX = 8
ROOT_HALF = 0.5 ** 0.5
COS8 = ((1, 1), (1, 2), (0, 0), (-1, 2), (-1, 1), (-1, 2), (0, 0), (1, 2))
SIN8 = ((0, 0), (1, 2), (1, 1), (1, 2), (0, 0), (-1, 2), (-1, 1), (-1, 2))


def _signed_sum(terms):
    acc = None
    for sign, x in terms:
        if acc is None:
            acc = x if sign > 0 else -x
        else:
            acc = acc + x if sign > 0 else acc - x
    return acc


def _fs_kernel(m_ref, twc_ref, tws_ref, va_ref, vb_ref, prev_ref, o_ref, br_ref, bi_ref):
    del prev_ref
    n_sub = va_ref.shape[0] // FFT_RADIX
    for n1 in range(FFT_RADIX):
        halves = [_unpack_pair(v_ref[pl.ds(n1, n_sub, stride=FFT_RADIX), :]) for v_ref in (va_ref, vb_ref)]
        z = jnp.concatenate([jnp.concatenate([h[part].astype(BF) for h in halves], axis=1)
                             for part in range(2)], axis=0)
        a = jnp.dot(m_ref[...], z, preferred_element_type=F32)
        ar, ai = a[:n_sub], a[n_sub:]
        if n1 == 0:
            br_ref[n1], bi_ref[n1] = ar, ai
        else:
            tc, ts = twc_ref[n1], tws_ref[n1]
            br_ref[n1] = tc * ar + ts * ai
            bi_ref[n1] = tc * ai - ts * ar
    for k1 in range(FFT_RADIX):
        unit, root = [], []
        for n1 in range(FFT_RADIX):
            m = (n1 * k1) % FFT_RADIX
            for (sign, cls), ref in ((COS8[m], br_ref), (SIN8[m], bi_ref)):
                if cls:
                    (unit if cls == 1 else root).append((sign, ref[n1]))
        acc = _signed_sum(unit)
        if root:
            acc = acc + ROOT_HALF * _signed_sum(root)
        o_ref[k1 * n_sub:(k1 + 1) * n_sub, :] = acc.astype(o_ref.dtype)


def _fourier_seq_fft(cfg, v, prev):
    t, d = v.shape
    seq = cfg.latent_len
    n_sub = seq // FFT_RADIX
    rb0 = cfg.n_prompt // seq
    cn, sn = _dft_tables(n_sub)
    m = (jnp.block([[cn, -sn], [-sn, -cn]]) * seq ** -0.5).astype(BF)
    ang = (lax.broadcasted_iota(I32, (FFT_RADIX, n_sub), 0) * lax.broadcasted_iota(I32, (FFT_RADIX, n_sub), 1)
           ).astype(F32) * (2.0 * jnp.pi / seq)
    twc = jnp.broadcast_to(jnp.cos(ang)[:, :, None], (FFT_RADIX, n_sub, 2 * LANES))
    tws = jnp.broadcast_to(jnp.sin(ang)[:, :, None], (FFT_RADIX, n_sub, 2 * LANES))
    return pl.pallas_call(
        _fs_kernel,
        grid=(cfg.n_latent // seq, d // (2 * LANES)),
        in_specs=[pl.BlockSpec((2 * n_sub, 2 * n_sub), lambda b, j: (0, 0)),
                  pl.BlockSpec((FFT_RADIX, n_sub, 2 * LANES), lambda b, j: (0, 0, 0)),
                  pl.BlockSpec((FFT_RADIX, n_sub, 2 * LANES), lambda b, j: (0, 0, 0)),
                  pl.BlockSpec((seq, LANES), lambda b, j: (rb0 + b, 2 * j)),
                  pl.BlockSpec((seq, LANES), lambda b, j: (rb0 + b, 2 * j + 1)),
                  pl.BlockSpec(memory_space=pl.ANY)],
        out_specs=pl.BlockSpec((seq, 2 * LANES), lambda b, j: (rb0 + b, j)),
        out_shape=jax.ShapeDtypeStruct((t, d), BF),
        scratch_shapes=[pltpu.VMEM((FFT_RADIX, n_sub, 2 * LANES), F32),
                        pltpu.VMEM((FFT_RADIX, n_sub, 2 * LANES), F32)],
        input_output_aliases={5: 0},
        compiler_params=_params(("arbitrary", "arbitrary")),
        name="fourier_seq_fft",
    )(m, twc, tws, v, v, prev)


def _dft_tables(n):
    j = lax.broadcasted_iota(I32, (n, n), 0)
    k = lax.broadcasted_iota(I32, (n, n), 1)
    ang = ((j * k) % n).astype(F32) * (2.0 * jnp.pi / n)
    return jnp.cos(ang), jnp.sin(ang)


def _fourier_mix(cfg, h, w_in):
    d = cfg.d
    gd = d // FOURIER_GROUPS
    cc, sc = _dft_tables(gd)
    cs = (jnp.concatenate([cc, sc], axis=1) * gd ** -0.5).astype(BF)
    v = _fourier_in(h, w_in, cs)
    cp, sp = _dft_tables(cfg.prompt_len)
    a_p = (jnp.stack([cp, -sp]) * cfg.prompt_len ** -0.5).astype(BF)
    f = _fourier_seq_dense(a_p, v, h, n_seq=cfg.n_prompt // cfg.prompt_len, seq_len=cfg.prompt_len)
    return _fourier_seq_fft(cfg, v, f)


def _softmax_pv(scores, values):
    m = functools.reduce(jnp.maximum, [jnp.max(s, axis=-1, keepdims=True) for s in scores])
    ps = [jnp.exp(s - m) for s in scores]
    l = functools.reduce(jnp.add, [jnp.sum(p, axis=-1, keepdims=True) for p in ps])
    o = functools.reduce(jnp.add, [jnp.dot(p.astype(BF), v, preferred_element_type=F32)
                                   for p, v in zip(ps, values)])
    return o / l


def _qk(q, k):
    return lax.dot_general(q, k, (((1,), (1,)), ((), ())), preferred_element_type=F32)


def _pa_kernel(q_ref, k_ref, v_ref, *rest, heads, dh):
    o_ref = rest[-1]
    for h in range(heads):
        sl = slice(h * dh, (h + 1) * dh)
        s = _qk(q_ref[:, sl], k_ref[:, sl].astype(BF))
        o_ref[:, sl] = _softmax_pv([s], [v_ref[:, sl].astype(BF)]).astype(o_ref.dtype)


def _prompt_attention(cfg, q, k, v, prev):
    lp, d = cfg.prompt_len, cfg.d
    hb = min(8, cfg.heads)
    w = hb * cfg.head_dim
    spec = pl.BlockSpec((lp, w), lambda b, g: (b, g))
    return pl.pallas_call(
        functools.partial(_pa_kernel, heads=hb, dh=cfg.head_dim),
        grid=(cfg.n_prompt // lp, d // w),
        in_specs=[spec, spec, spec, pl.BlockSpec(memory_space=pl.ANY)],
        out_specs=spec,
        out_shape=jax.ShapeDtypeStruct(prev.shape, BF),
        input_output_aliases={3: 0},
        compiler_params=_params(("arbitrary", "arbitrary")),
        name="prompt_attention",
    )(q, k, v, prev)


def _rx_kernel(r_ref, o_ref):
    r = r_ref[...]
    hi = r.astype(BF)
    r1 = r - hi.astype(F32)
    mid = r1.astype(BF)
    lo = (r1 - mid.astype(F32)).astype(BF)
    n = o_ref.shape[1]
    idx = lax.broadcasted_iota(I32, (ROUTE_LANES, n), 1)
    row = lax.broadcasted_iota(I32, (ROUTE_LANES, n), 0)
    qc, kc = idx >> 7, idx & (GRID_W - 1)
    dc = jnp.clip(kc - qc + NA_COLS - 1, 0, 2 * NA_COLS - 2)
    onehot = jnp.where(row == dc, 1.0, 0.0).astype(BF)
    t = (jnp.dot(hi, onehot, preferred_element_type=F32)
         + jnp.dot(mid, onehot, preferred_element_type=F32)
         + jnp.dot(lo, onehot, preferred_element_type=F32))
    col = lax.broadcasted_iota(I32, (1, n), 1)
    qc1, kc1 = col >> 7, col & (GRID_W - 1)
    c0 = jnp.clip(qc1 - NA_COLS // 2, 0, GRID_W - NA_COLS)
    o_ref[...] = jnp.where((kc1 >= c0) & (kc1 < c0 + NA_COLS), t, MASKED)


def _expand_rpb(rpb):
    heads, nr, nc = rpb.shape
    rows = heads * nr
    tr = rows // 4
    r2 = jnp.pad(rpb.reshape(rows, nc), ((0, 0), (0, ROUTE_LANES - nc)))
    out = pl.pallas_call(
        _rx_kernel,
        grid=(rows // tr,),
        in_specs=[pl.BlockSpec((tr, ROUTE_LANES), lambda i: (i, 0))],
        out_specs=pl.BlockSpec((tr, GRID_W * 2 * GRID_W), lambda i: (i, 0)),
        out_shape=jax.ShapeDtypeStruct((rows, GRID_W * 2 * GRID_W), F32),
        compiler_params=_params(("arbitrary",)),
        name="rpb_expand",
    )(r2)
    return out.reshape(heads, nr, GRID_W, 2 * GRID_W)


NA_QROWS = 4
NA_KROWS = NA_QROWS + NA_ROWS
NA_VARIANTS = ((0, 0), (2 * NA_QROWS, NA_QROWS), (GRID_W - NA_QROWS, GRID_W - NA_KROWS))


NA_UNROLL = 4


def _na_kernel(q_ref, k_ref, v_ref, kc_ref, vc_ref, t_ref, prev_ref, o_ref, bias_ref):
    del prev_ref
    w = GRID_W

    @pl.when(pl.program_id(1) == 0)
    def _build_bias():
        left = lax.broadcasted_iota(I32, (w, 2 * w), 1) < w
        masked = jnp.full((w, 2 * w), MASKED, F32)

        def table(r, kr):
            rs = min(max(r - NA_ROWS // 2, 0), w - NA_ROWS)
            return t_ref[0, kr - r + NA_ROWS - 1] if rs <= kr < rs + NA_ROWS else masked

        for var, (r0, k0) in enumerate(NA_VARIANTS):
            for i in range(NA_QROWS):
                for j in range(0, NA_KROWS, 2):
                    pair = jnp.where(left, table(r0 + i, k0 + j), table(r0 + i, k0 + j + 1))
                    bias_ref[var, i * w:(i + 1) * w, j * w:(j + 2) * w] = pair

    kc = kc_ref[0].astype(BF)
    vc = vc_ref[0].astype(BF)
    nq = NA_QROWS * w
    nk = NA_KROWS * w
    n_blocks = w // NA_QROWS

    def block(qb, carry):
        var = jnp.where(qb == 0, 0, jnp.where(qb == n_blocks - 1, 2, 1))
        k0 = jnp.clip(qb * NA_QROWS - NA_ROWS // 2, 0, w - NA_KROWS)
        q0 = pl.multiple_of(qb * nq, nq)
        ks = pl.multiple_of(k0 * w, w)
        q = q_ref[pl.ds(q0, nq), :]
        s_lat = _qk(q, k_ref[pl.ds(ks, nk), :]) + bias_ref[var]
        s_ctx = _qk(q, kc)
        o = _softmax_pv([s_lat, s_ctx], [v_ref[pl.ds(ks, nk), :], vc])
        o_ref[pl.ds(q0, nq), :] = o.astype(o_ref.dtype)
        return carry

    def block_group(g, carry):
        for u in range(NA_UNROLL):
            block(g * NA_UNROLL + u, carry)
        return carry

    lax.fori_loop(0, n_blocks // NA_UNROLL, block_group, 0)


def _latent_attention(cfg, qkv, cache_k, cache_v, t_bias, prev):
    d, dh, heads, ls = cfg.d, cfg.head_dim, cfg.heads, cfg.latent_len
    nb = cfg.n_latent // ls
    rb0 = cfg.n_prompt // ls
    w = GRID_W
    return pl.pallas_call(
        _na_kernel,
        grid=(heads, nb),
        in_specs=[pl.BlockSpec((ls, dh), lambda h, b: (b, h)),
                  pl.BlockSpec((ls, dh), lambda h, b: (b, heads + h)),
                  pl.BlockSpec((ls, dh), lambda h, b: (b, 2 * heads + h)),
                  pl.BlockSpec((1, cfg.past, dh), lambda h, b: (b, 0, h)),
                  pl.BlockSpec((1, cfg.past, dh), lambda h, b: (b, 0, h)),
                  pl.BlockSpec((1, 2 * NA_ROWS - 1, w, 2 * w), lambda h, b: (h, 0, 0, 0)),
                  pl.BlockSpec(memory_space=pl.ANY)],
        out_specs=pl.BlockSpec((ls, dh), lambda h, b: (rb0 + b, h)),
        out_shape=jax.ShapeDtypeStruct(prev.shape, BF),
        scratch_shapes=[pltpu.VMEM((len(NA_VARIANTS), NA_QROWS * w, NA_KROWS * w), F32)],
        input_output_aliases={6: 0},
        compiler_params=_params(("arbitrary", "arbitrary")),
        name="latent_attention",
    )(qkv, qkv, qkv, cache_k, cache_v, t_bias, prev)


U32 = jnp.uint32
HIGH_HALF = 0xFFFF0000


def _slab(d):
    rows = d // (2 * LANES)
    return rows, -(-rows // SUBLANES) * SUBLANES + SLAB_PAD


def _bf16_bits(x):
    return lax.bitcast_convert_type(x.astype(BF).astype(F32), U32)


def _pack_pair(lo_bits, hi_bits):
    return (hi_bits & jnp.uint32(HIGH_HALF)) | (lo_bits >> 16)


def _unpack_pair(w):
    return (lax.bitcast_convert_type(w << 16, F32),
            lax.bitcast_convert_type(w & jnp.uint32(HIGH_HALF), F32))


def _store_slabs(ref, bits):
    n, d = bits.shape
    rows, pitch = _slab(d)
    for r in range(rows):
        lo = bits[:, r * LANES:(r + 1) * LANES]
        hi = bits[:, d // 2 + r * LANES:d // 2 + (r + 1) * LANES]
        ref[pl.ds(r, n, stride=pitch), :] = _pack_pair(lo, hi)
    for r in range(rows, pitch):
        ref[pl.ds(r, n, stride=pitch), :] = jnp.zeros((n, LANES), U32)


def _load_slab_piece(ref, lead, n, pitch, r):
    return _unpack_pair(ref[lead + (pl.ds(r, n, stride=pitch), slice(None))])


def _nmr_kernel(x_ref, g_ref, sc_ref, sh_ref, wr_ref, br_ref, hs_ref, ri_ref, rg_ref, cnt_ref, base_ref):
    tm, d = x_ref.shape

    @pl.when(pl.program_id(0) == 0)
    def _init():
        base_ref[...] = jnp.zeros_like(base_ref)

    h = _norm_mod(x_ref[...], g_ref[...], sc_ref[0], sh_ref[0])
    h_hi = h.astype(BF)
    h_hi32 = h_hi.astype(F32)
    _store_slabs(hs_ref, lax.bitcast_convert_type(h_hi32, U32))

    h_lo = (h - h_hi32).astype(BF)
    wr = wr_ref[...]
    w_hi = wr.astype(BF)
    w_lo = (wr - w_hi.astype(F32)).astype(BF)
    logits = (jnp.dot(h_hi, w_hi, preferred_element_type=F32)
              + jnp.dot(h_lo, w_hi, preferred_element_type=F32)
              + jnp.dot(h_hi, w_lo, preferred_element_type=F32)) + br_ref[...]

    lane = lax.broadcasted_iota(I32, (tm, ROUTE_LANES), 1)
    lane_f = lane.astype(F32)

    def first_argmax(v):
        m = jnp.max(v, axis=-1, keepdims=True)
        first = jnp.min(jnp.where(v == m, lane_f, float(ROUTE_LANES)), axis=-1, keepdims=True)
        return m, first.astype(I32)

    gl = jnp.where(lane < N_EXPERT_GROUPS, logits, MASKED)
    gmax, gi = first_argmax(gl)
    g_w = 1.0 / jnp.sum(jnp.exp(gl - gmax), axis=-1, keepdims=True)
    e0 = N_EXPERT_GROUPS + gi * EXPERTS_PER_GROUP
    el = jnp.where((lane >= e0) & (lane < e0 + EXPERTS_PER_GROUP), logits, MASKED)
    m1, i1 = first_argmax(el)
    m2, i2 = first_argmax(jnp.where(lane == i1, MASKED, el))
    tt = jnp.exp(m2 - m1)
    w1 = 1.0 / (1.0 + tt)
    eid0, eid1 = i1 - N_EXPERT_GROUPS, i2 - N_EXPERT_GROUPS

    oh = jnp.where((lane == eid0) | (lane == eid1), 1.0, 0.0)
    row = lax.broadcasted_iota(I32, (tm, tm), 0)
    colt = lax.broadcasted_iota(I32, (tm, tm), 1)
    earlier = jnp.where(row > colt, 1.0, 0.0).astype(BF)
    before = jnp.dot(earlier, oh.astype(BF), preferred_element_type=F32) + base_ref[0:1, :]
    rank0 = jnp.sum(jnp.where(lane == eid0, before, 0.0), axis=-1, keepdims=True).astype(I32)
    rank1 = jnp.sum(jnp.where(lane == eid1, before, 0.0), axis=-1, keepdims=True).astype(I32)
    base_ref[...] = base_ref[...] + jnp.sum(oh, axis=0, keepdims=True)

    ri_ref[...] = jnp.where(lane == 0, eid0, jnp.where(lane == 1, eid1,
                            jnp.where(lane == 2, rank0, jnp.where(lane == 3, rank1, 0))))
    rg_ref[...] = jnp.where(lane == 0, g_w * w1, jnp.where(lane == 1, g_w * (tt * w1), 0.0))
    cnt_ref[...] = base_ref[...]


def _norm_route(cfg, x, g, mods, layer, w_route, b_route):
    t, d = x.shape
    tm = 256
    _, pitch = _slab(d)
    return pl.pallas_call(
        _nmr_kernel,
        grid=(t // tm,),
        in_specs=[pl.BlockSpec((tm, d), lambda i: (i, 0)),
                  pl.BlockSpec((1, d), lambda i: (0, 0)),
                  _mod_spec(cfg, layer, 4, tm),
                  _mod_spec(cfg, layer, 3, tm),
                  pl.BlockSpec((d, ROUTE_LANES), lambda i: (0, 0)),
                  pl.BlockSpec((1, ROUTE_LANES), lambda i: (0, 0))],
        out_specs=[pl.BlockSpec((tm * pitch, LANES), lambda i: (i, 0)),
                   pl.BlockSpec((tm, ROUTE_LANES), lambda i: (i, 0)),
                   pl.BlockSpec((tm, ROUTE_LANES), lambda i: (i, 0)),
                   pl.BlockSpec((SUBLANES, ROUTE_LANES), lambda i: (0, 0))],
        out_shape=[jax.ShapeDtypeStruct((t * pitch, LANES), U32),
                   jax.ShapeDtypeStruct((t, ROUTE_LANES), I32),
                   jax.ShapeDtypeStruct((t, ROUTE_LANES), F32),
                   jax.ShapeDtypeStruct((SUBLANES, ROUTE_LANES), F32)],
        scratch_shapes=[pltpu.VMEM((SUBLANES, ROUTE_LANES), F32)],
        compiler_params=_params(("arbitrary",)),
        name="norm_route",
    )(x, g.reshape(1, d), mods, mods, w_route, b_route)


MOVE_TOKENS = 256


def _wait_bytes_of(buf_ref, sem):
    pltpu.make_async_copy(buf_ref, buf_ref, sem).wait()


def _ds_kernel(dest_ref, zstart_ref, hs_ref, xs_ref, zbuf_ref, sem, zsem):
    n = MOVE_TOKENS
    pitch = hs_ref.shape[0] // n
    bm = zbuf_ref.shape[0]
    n_blocks = xs_ref.shape[0] // bm

    @pl.when(pl.program_id(0) == 0)
    def _zero_fill():
        zbuf_ref[...] = jnp.zeros_like(zbuf_ref)
        n_used = zstart_ref[0, 0, N_EXPERTS]
        targets = [(zstart_ref[0, 0, e] >= 0, zstart_ref[0, 0, e]) for e in range(N_EXPERTS)]
        targets += [(n_used + j < n_blocks, (n_used + j) * bm) for j in range(N_EXPERTS)]
        for wait in (False, True):
            for live, start in targets:
                @pl.when(live)
                def _():
                    copy = pltpu.make_async_copy(zbuf_ref, xs_ref.at[pl.ds(start, bm)], zsem)
                    copy.wait() if wait else copy.start()

    def issue(tk, carry):
        src = hs_ref.at[pl.ds(pl.multiple_of(tk * pitch, SUBLANES), pitch), :]
        for k in range(2):
            pltpu.make_async_copy(src, xs_ref.at[dest_ref[0, 0, 2 * tk + k]], sem).start()
        return carry

    lax.fori_loop(0, n, issue, 0)
    _wait_bytes_of(xs_ref.at[pl.ds(0, 2 * n)], sem)


def _dispatch(hs, dest, zero_plan, n_slots, bm):
    n = MOVE_TOKENS
    t = dest.shape[0] // 2
    pitch = hs.shape[0] // t
    return pl.pallas_call(
        _ds_kernel,
        grid=(t // n,),
        in_specs=[pl.BlockSpec((1, 1, 2 * n), lambda i: (i, 0, 0), memory_space=pltpu.SMEM),
                  pl.BlockSpec((1, 1, N_EXPERTS + 1), lambda i: (0, 0, 0), memory_space=pltpu.SMEM),
                  pl.BlockSpec((n * pitch, LANES), lambda i: (i, 0))],
        out_specs=pl.BlockSpec(memory_space=pl.ANY),
        out_shape=jax.ShapeDtypeStruct((n_slots, pitch, LANES), U32),
        scratch_shapes=[pltpu.VMEM((bm, pitch, LANES), U32), pltpu.SemaphoreType.DMA, pltpu.SemaphoreType.DMA],
        compiler_params=_params(("arbitrary",)),
        name="moe_dispatch",
    )(dest.reshape(t // n, 1, 2 * n), zero_plan.reshape(1, 1, N_EXPERTS + 1), hs)


def _ex_kernel(blk_e_ref, n_used_ref, x_ref, wg_ref, wu_ref, wd_ref, o_ref, xb_ref):
    del blk_e_ref
    bm, d = xb_ref.shape
    rows, pitch = _slab(d)

    @pl.when(pl.program_id(0) < n_used_ref[0])
    def _compute():
        for r in range(rows):
            lo, hi = _load_slab_piece(x_ref, (), bm, pitch, r)
            xb_ref[:, r * LANES:(r + 1) * LANES] = lo.astype(BF)
            xb_ref[:, d // 2 + r * LANES:d // 2 + (r + 1) * LANES] = hi.astype(BF)
        x = xb_ref[...]
        g = jnp.dot(x, wg_ref[0], preferred_element_type=F32)
        u = jnp.dot(x, wu_ref[0], preferred_element_type=F32)
        y = jnp.dot((_silu(g) * u).astype(BF), wd_ref[0], preferred_element_type=F32)
        _store_slabs(o_ref, _bf16_bits(y))


def _experts(cfg, xs, blk_e, n_used, wg, wu, wd):
    n_slots, sr, sw = xs.shape
    d, ff, bm = cfg.d, cfg.ff, cfg.moe_block
    nb = n_slots // bm

    def blk(b, be, nu):
        return jnp.minimum(b, nu[0] - 1)

    grid_spec = pltpu.PrefetchScalarGridSpec(
        num_scalar_prefetch=2,
        grid=(nb,),
        in_specs=[pl.BlockSpec((bm * sr, sw), lambda b, be, nu: (blk(b, be, nu), 0)),
                  pl.BlockSpec((1, d, ff), lambda b, be, nu: (be[blk(b, be, nu)], 0, 0)),
                  pl.BlockSpec((1, d, ff), lambda b, be, nu: (be[blk(b, be, nu)], 0, 0)),
                  pl.BlockSpec((1, ff, d), lambda b, be, nu: (be[blk(b, be, nu)], 0, 0))],
        out_specs=pl.BlockSpec((bm * sr, sw), lambda b, be, nu: (blk(b, be, nu), 0)),
        scratch_shapes=[pltpu.VMEM((bm, d), BF)],
    )
    ys = pl.pallas_call(
        _ex_kernel,
        grid_spec=grid_spec,
        out_shape=jax.ShapeDtypeStruct((n_slots * sr, sw), U32),
        input_output_aliases={2: 0},
        compiler_params=_params(("arbitrary",)),
        name="moe_experts",
    )(blk_e, n_used, xs.reshape(n_slots * sr, sw), wg, wu, wd)
    return ys.reshape(n_slots, sr, sw)


def _gather_combine(dcur_ref, dnext_ref, x_ref, ys_ref, rg_ref, gt_ref, o_ref, ybuf_ref, sems):
    tm, d = x_ref.shape
    rows, pitch = _slab(d)
    copy_rows = pitch - SLAB_PAD
    i, n_steps = pl.program_id(0), pl.num_programs(0)

    def start_gather(dref, slot):
        def issue(tk, carry):
            for k in range(2):
                row0 = pl.multiple_of((k * tm + tk) * pitch, SUBLANES)
                pltpu.make_async_copy(ys_ref.at[dref[0, 0, 2 * tk + k], pl.ds(0, copy_rows), :],
                                      ybuf_ref.at[slot, pl.ds(row0, copy_rows), :], sems.at[slot]).start()
            return carry
        lax.fori_loop(0, tm, issue, 0)

    @pl.when(i == 0)
    def _first():
        start_gather(dcur_ref, 0)

    @pl.when(i + 1 < n_steps)
    def _next():
        start_gather(dnext_ref, (i + 1) % 2)

    slot = i % 2
    _wait_bytes_of(ybuf_ref.at[slot, pl.ds(0, 2 * tm * copy_rows), :], sems.at[slot])
    g0, g1 = rg_ref[:, 0:1], rg_ref[:, 1:2]
    for r in range(rows):
        lo0, hi0 = _load_slab_piece(ybuf_ref, (slot,), tm, pitch, r)
        lo1, hi1 = _load_slab_piece(ybuf_ref, (slot,), tm, pitch, tm * pitch + r)
        for c0, y0, y1 in ((r * LANES, lo0, lo1), (d // 2 + r * LANES, hi0, hi1)):
            sl = slice(c0, c0 + LANES)
            o_ref[:, sl] = x_ref[:, sl] + gt_ref[0][:, sl] * (y0 * g0 + y1 * g1)


def _cb_next_kernel(dcur_ref, dnext_ref, x_ref, ys_ref, rg_ref, gt_ref, g_ref, sc_ref, sh_ref,
                    o_ref, hn_ref, ybuf_ref, sems):
    _gather_combine(dcur_ref, dnext_ref, x_ref, ys_ref, rg_ref, gt_ref, o_ref, ybuf_ref, sems)
    hn_ref[...] = _norm_mod(o_ref[...], g_ref[...], sc_ref[0], sh_ref[0]).astype(hn_ref.dtype)


def _cb_final_kernel(dcur_ref, dnext_ref, x_ref, ys_ref, rg_ref, gt_ref, g_ref,
                     yp_ref, yl_ref, ybuf_ref, sems, xn_ref, *, prompt_tiles):
    _gather_combine(dcur_ref, dnext_ref, x_ref, ys_ref, rg_ref, gt_ref, xn_ref, ybuf_ref, sems)
    xn = xn_ref[...]
    y = xn * lax.rsqrt(jnp.mean(xn * xn, axis=-1, keepdims=True) + NORM_EPS) * g_ref[...]
    i = pl.program_id(0)

    @pl.when(i < prompt_tiles)
    def _prompt():
        yp_ref[...] = y

    @pl.when(i >= prompt_tiles)
    def _latent():
        yl_ref[...] = y


def _combine(cfg, x, ys, dest, rg, mods, layer, g, next_layer):
    t, d = x.shape
    tm = MOVE_TOKENS
    _, pitch = _slab(d)
    n = t // tm
    dest3 = dest.reshape(n, 1, 2 * tm)
    row = pl.BlockSpec((tm, d), lambda i: (i, 0))
    in_specs = [pl.BlockSpec((1, 1, 2 * tm), lambda i: (i, 0, 0), memory_space=pltpu.SMEM),
                pl.BlockSpec((1, 1, 2 * tm), lambda i: (jnp.minimum(i + 1, n - 1), 0, 0),
                             memory_space=pltpu.SMEM),
                row,
                pl.BlockSpec(memory_space=pl.ANY),
                pl.BlockSpec((tm, ROUTE_LANES), lambda i: (i, 0)),
                _mod_spec(cfg, layer, 5, tm),
                pl.BlockSpec((1, d), lambda i: (0, 0))]
    scratch = [pltpu.VMEM((2, 2 * tm * pitch, LANES), U32), pltpu.SemaphoreType.DMA((2,))]
    args = [dest3, dest3, x, ys, rg, mods, g.reshape(1, d)]
    if next_layer is not None:
        return pl.pallas_call(
            _cb_next_kernel,
            grid=(n,),
            in_specs=in_specs + [_mod_spec(cfg, next_layer, 1, tm), _mod_spec(cfg, next_layer, 0, tm)],
            out_specs=[row, row],
            out_shape=[jax.ShapeDtypeStruct((t, d), F32), jax.ShapeDtypeStruct((t, d), BF)],
            scratch_shapes=scratch,
            input_output_aliases={2: 0},
            compiler_params=_params(("arbitrary",)),
            name="moe_combine_norm",
        )(*args, mods, mods)
    pt = cfg.n_prompt // tm
    return pl.pallas_call(
        functools.partial(_cb_final_kernel, prompt_tiles=pt),
        grid=(n,),
        in_specs=in_specs,
        out_specs=[pl.BlockSpec((tm, d), lambda i: (jnp.minimum(i, pt - 1), 0)),
                   pl.BlockSpec((tm, d), lambda i: (jnp.maximum(i - pt, 0), 0))],
        out_shape=[jax.ShapeDtypeStruct((cfg.n_prompt, d), F32), jax.ShapeDtypeStruct((cfg.n_latent, d), F32)],
        scratch_shapes=scratch + [pltpu.VMEM((tm, d), F32)],
        compiler_params=_params(("arbitrary",)),
        name="moe_combine_final",
    )(*args)


def _moe(cfg, x, g, mods, layer, w_grp, b_grp, w_exp, b_exp, wg, wu, wd, g_after, next_layer):
    t, d = x.shape
    bm = cfg.moe_block
    pad = ROUTE_LANES - N_EXPERT_GROUPS - N_EXPERTS
    w_route = jnp.pad(jnp.concatenate([w_grp, w_exp], axis=1), ((0, 0), (0, pad)))
    b_route = jnp.pad(jnp.concatenate([b_grp, b_exp]), (0, pad)).reshape(1, ROUTE_LANES)
    hs, ri, rg, cnt = _norm_route(cfg, x, g, mods, layer, w_route, b_route)

    counts = cnt[0, :N_EXPERTS].astype(I32)
    padded = (counts + bm - 1) // bm * bm
    ends = jnp.cumsum(padded)
    starts = ends - padded
    eid, rank = ri[:, 0:2], ri[:, 2:4]
    start_of = jnp.sum(jnp.where(eid[:, :, None] == jnp.arange(N_EXPERTS, dtype=I32), starts, 0), axis=-1)
    dest = (start_of + rank).reshape(-1)
    n_blocks = -(-2 * t // bm) + N_EXPERTS
    blk_e = jnp.minimum(jnp.sum(ends[None, :] <= (jnp.arange(n_blocks, dtype=I32) * bm)[:, None], axis=1),
                        N_EXPERTS - 1).astype(I32) + layer * N_EXPERTS
    n_used = (ends[-1:] // bm).astype(I32)
    zero_plan = jnp.concatenate([jnp.where(counts > 0, ends - bm, -1).astype(I32), n_used])

    xs = _dispatch(hs, dest, zero_plan, n_blocks * bm, bm)
    ys = _experts(cfg, xs, blk_e, n_used, wg, wu, wd)
    return _combine(cfg, x, ys, dest, rg, mods, layer, g_after, next_layer)


def _forward(cfg, x_prompt, x_sample, cache_k, cache_v, c, c_ctx, w_ada, b_ada, norm_mix_g, norm_ffn_g,
             fourier_w_in, fourier_w_out, na_w_qkv, na_w_out, na_rpb,
             router_grp_w, router_grp_b, router_exp_w, router_exp_b,
             expert_w_gate, expert_w_up, expert_w_down, final_norm_g):
    d = cfg.d
    depth = w_ada.shape[0]
    xp, xl = x_prompt.reshape(cfg.n_prompt, d), x_sample.reshape(cfg.n_latent, d)

    nb_lat = cfg.n_latent // cfg.latent_len
    cond = jnp.concatenate([c_ctx[None], c, jnp.zeros((COND_ROWS - 1 - nb_lat, d), F32)], axis=0)
    mods = _ada(cond, w_ada, b_ada).reshape(depth * COND_ROWS * N_MODS, 1, d)

    n_all = depth * N_EXPERTS
    wg_all = expert_w_gate.astype(BF).reshape(n_all, d, cfg.ff)
    wu_all = expert_w_up.astype(BF).reshape(n_all, d, cfg.ff)
    wd_all = expert_w_down.astype(BF).reshape(n_all, cfg.ff, d)

    x = (xp, xl)
    h = _nm(cfg, xp, xl, norm_mix_g[0], mods, 0, 1, 0)
    new_k, new_v = [], []
    for i in range(depth):
        j = i // 2
        if i % 2 == 0:
            f = _fourier_mix(cfg, h, fourier_w_in[j].astype(BF))
            x = _mm_residual(cfg, f, fourier_w_out[j].astype(BF), x, mods, i, 2)
        else:
            w_qkv = na_w_qkv[j].astype(BF)
            scale = cfg.head_dim ** -0.5
            q_p = _mm(h, w_qkv, row0=0, rows=cfg.n_prompt, col0=0, cols=d, out_dtype=BF,
                      scaled_cols=d, scale=scale)
            k_p = _mm(h, w_qkv, row0=0, rows=cfg.n_prompt, col0=d, cols=d, out_dtype=F32)
            v_p = _mm(h, w_qkv, row0=0, rows=cfg.n_prompt, col0=2 * d, cols=d, out_dtype=F32)
            qkv_l = _mm(h, w_qkv, row0=cfg.n_prompt, rows=cfg.n_latent, col0=0, cols=3 * d, out_dtype=BF,
                        scaled_cols=d, scale=scale)
            new_k.append(k_p)
            new_v.append(v_p)
            o = _prompt_attention(cfg, q_p, k_p, v_p, h)
            t_bias = _expand_rpb(na_rpb[j])
            ck = cache_k[:, j].reshape(nb_lat, cfg.past, d)
            cv = cache_v[:, j].reshape(nb_lat, cfg.past, d)
            o = _latent_attention(cfg, qkv_l, ck, cv, t_bias, o)
            x = _mm_residual(cfg, o, na_w_out[j].astype(BF), x, mods, i, 2)
        last = i == depth - 1
        x, h = _moe(cfg, x, norm_ffn_g[i], mods, i, router_grp_w[i], router_grp_b[i], router_exp_w[i],
                    router_exp_b[i], wg_all, wu_all, wd_all,
                    final_norm_g if last else norm_mix_g[i + 1], None if last else i + 1)

    y_prompt, y_sample = x.reshape(x_prompt.shape), h.reshape(x_sample.shape)
    nbp = x_prompt.shape[0]
    kv_shape = (nbp, cfg.prompt_len, cfg.heads, cfg.head_dim)
    new_k = jnp.stack([a.reshape(kv_shape) for a in new_k], axis=1)
    new_v = jnp.stack([a.reshape(kv_shape) for a in new_v], axis=1)
    return (y_prompt, y_sample, new_k, new_v)


def kernel(x_prompt, x_sample, cache_k, cache_v, c, c_ctx, w_ada, b_ada, norm_mix_g, norm_ffn_g,
           fourier_w_in, fourier_w_out, na_w_qkv, na_w_out, na_rpb,
           router_grp_w, router_grp_b, router_exp_w, router_exp_b,
           expert_w_gate, expert_w_up, expert_w_down, final_norm_g):
    bp, lp, d = x_prompt.shape
    bl, ll, _ = x_sample.shape
    heads = na_rpb.shape[1]
    cfg = Cfg(d=d, n_prompt=bp * lp, prompt_len=lp, n_latent=bl * ll, latent_len=ll, heads=heads,
              head_dim=d // heads, past=cache_k.shape[2], ff=expert_w_gate.shape[-1], moe_block=256)
    return _forward(cfg, x_prompt, x_sample, cache_k, cache_v, c, c_ctx, w_ada, b_ada, norm_mix_g,
                    norm_ffn_g, fourier_w_in, fourier_w_out, na_w_qkv, na_w_out, na_rpb,
                    router_grp_w, router_grp_b, router_exp_w, router_exp_b,
                    expert_w_gate, expert_w_up, expert_w_down, final_norm_g)
```

```python
import collections
import functools

import jax
import jax.numpy as jnp
from jax import lax
from jax.experimental import pallas as pl
from jax.experimental.pallas import tpu as pltpu

BF = jnp.bfloat16
F32 = jnp.float32
I32 = jnp.int32

NORM_EPS = 1e-6
MASKED = -1e30
GRID_W = 64
NA_ROWS = 8
NA_COLS = 16
FOURIER_GROUPS = 8
EXPERTS_PER_GROUP = 8
N_EXPERT_GROUPS = 4
N_EXPERTS = N_EXPERT_GROUPS * EXPERTS_PER_GROUP
N_MODS = 6
COND_ROWS = 8
LANES = 128
SUBLANES = 8
SLAB_PAD = 8
ROUTE_LANES = LANES
V7X_VMEM_LIMIT = 56 * 2**20

Cfg = collections.namedtuple(
    "Cfg", "d n_prompt prompt_len n_latent latent_len heads head_dim past ff moe_block")


def _params(sem, vmem=V7X_VMEM_LIMIT):
    return pltpu.CompilerParams(dimension_semantics=sem, vmem_limit_bytes=vmem)


def _silu(x):
    return x / (1.0 + jnp.exp(-x))


def _group_of_tile(i, tm, cfg):
    start = i * tm
    return jnp.where(start < cfg.n_prompt, 0, 1 + (start - cfg.n_prompt) // cfg.latent_len)


def _mod_spec(cfg, layer, which, tm, width=None, col=False):
    width = cfg.d if width is None else width
    base = layer * COND_ROWS * N_MODS + which
    if col:
        return pl.BlockSpec((1, 1, width), lambda i, j: (base + _group_of_tile(i, tm, cfg) * N_MODS, 0, j))
    return pl.BlockSpec((1, 1, width), lambda i: (base + _group_of_tile(i, tm, cfg) * N_MODS, 0, 0))


def _ada_kernel(c_ref, w_ref, b_ref, o_ref):
    s = _silu(c_ref[...]).astype(BF)
    o_ref[0] = jnp.dot(s, w_ref[0].astype(BF), preferred_element_type=F32) + b_ref[0]


def _ada(cond, w_ada, b_ada):
    depth, d, n = w_ada.shape
    tn = min(512, n)
    return pl.pallas_call(
        _ada_kernel,
        grid=(depth, n // tn),
        in_specs=[pl.BlockSpec((COND_ROWS, d), lambda l, j: (0, 0)),
                  pl.BlockSpec((1, d, tn), lambda l, j: (l, 0, j)),
                  pl.BlockSpec((1, 1, tn), lambda l, j: (l, 0, j))],
        out_specs=pl.BlockSpec((1, COND_ROWS, tn), lambda l, j: (l, 0, j)),
        out_shape=jax.ShapeDtypeStruct((depth, COND_ROWS, n), F32),
        compiler_params=_params(("arbitrary", "arbitrary")),
        name="ada",
    )(cond, w_ada, b_ada.reshape(depth, 1, n))


def _norm_mod(x, g, sc, sh):
    y = x * lax.rsqrt(jnp.mean(x * x, axis=-1, keepdims=True) + NORM_EPS) * g
    return y * (1.0 + sc) + sh


def _split_specs(cfg, bm, width, two_d):
    pt = cfg.n_prompt // bm
    if two_d:
        return pt, [pl.BlockSpec((bm, width), lambda i, j: (jnp.minimum(i, pt - 1), j)),
                    pl.BlockSpec((bm, width), lambda i, j: (jnp.maximum(i - pt, 0), j))]
    return pt, [pl.BlockSpec((bm, width), lambda i: (jnp.minimum(i, pt - 1), 0)),
                pl.BlockSpec((bm, width), lambda i: (jnp.maximum(i - pt, 0), 0))]


def _nm_kernel(xp_ref, xl_ref, g_ref, sc_ref, sh_ref, o_ref, *, prompt_tiles):
    i = pl.program_id(0)

    @pl.when(i < prompt_tiles)
    def _prompt():
        o_ref[...] = _norm_mod(xp_ref[...], g_ref[...], sc_ref[0], sh_ref[0]).astype(o_ref.dtype)

    @pl.when(i >= prompt_tiles)
    def _latent():
        o_ref[...] = _norm_mod(xl_ref[...], g_ref[...], sc_ref[0], sh_ref[0]).astype(o_ref.dtype)


def _nm(cfg, xp, xl, g, mods, layer, sc_k, sh_k):
    d = cfg.d
    t = cfg.n_prompt + cfg.n_latent
    tm = 512
    pt, x_specs = _split_specs(cfg, tm, d, False)
    return pl.pallas_call(
        functools.partial(_nm_kernel, prompt_tiles=pt),
        grid=(t // tm,),
        in_specs=x_specs + [pl.BlockSpec((1, d), lambda i: (0, 0)),
                            _mod_spec(cfg, layer, sc_k, tm),
                            _mod_spec(cfg, layer, sh_k, tm)],
        out_specs=pl.BlockSpec((tm, d), lambda i: (i, 0)),
        out_shape=jax.ShapeDtypeStruct((t, d), BF),
        compiler_params=_params(("arbitrary",)),
        name="norm_mod",
    )(xp, xl, g.reshape(1, d), mods, mods)


def _mm_kernel(x_ref, w_ref, o_ref, *, n_scaled, scale):
    acc = jnp.dot(x_ref[...], w_ref[...], preferred_element_type=F32)
    if n_scaled:
        acc = acc * jnp.where(pl.program_id(1) < n_scaled, scale, 1.0)
    o_ref[...] = acc.astype(o_ref.dtype)


def _mm(x, w, *, row0, rows, col0, cols, out_dtype, scaled_cols=0, scale=1.0):
    k = x.shape[1]
    bm, bn = min(1024, rows), min(512, cols)
    r0, c0 = row0 // bm, col0 // bn
    return pl.pallas_call(
        functools.partial(_mm_kernel, n_scaled=scaled_cols // bn, scale=scale),
        grid=(rows // bm, cols // bn),
        in_specs=[pl.BlockSpec((bm, k), lambda i, j: (r0 + i, 0)),
                  pl.BlockSpec((k, bn), lambda i, j: (0, c0 + j))],
        out_specs=pl.BlockSpec((bm, bn), lambda i, j: (i, j)),
        out_shape=jax.ShapeDtypeStruct((rows, cols), out_dtype),
        compiler_params=_params(("arbitrary", "arbitrary")),
        name="matmul",
    )(x, w)


def _mr_kernel(x_ref, w_ref, r_ref, g_ref, o_ref):
    acc = jnp.dot(x_ref[...], w_ref[...], preferred_element_type=F32)
    o_ref[...] = r_ref[...] + g_ref[0] * acc


def _mr_split_kernel(x_ref, w_ref, rp_ref, rl_ref, g_ref, o_ref, *, prompt_tiles):
    acc = g_ref[0] * jnp.dot(x_ref[...], w_ref[...], preferred_element_type=F32)
    i = pl.program_id(0)

    @pl.when(i < prompt_tiles)
    def _prompt():
        o_ref[...] = rp_ref[...] + acc

    @pl.when(i >= prompt_tiles)
    def _latent():
        o_ref[...] = rl_ref[...] + acc


def _mm_residual(cfg, x, w, res, mods, layer, gate_k):
    t, k = x.shape
    n = w.shape[1]
    bm, bn = 1024, min(512, n)
    common = dict(
        grid=(t // bm, n // bn),
        out_specs=pl.BlockSpec((bm, bn), lambda i, j: (i, j)),
        out_shape=jax.ShapeDtypeStruct((t, n), F32),
        compiler_params=_params(("arbitrary", "arbitrary")),
        name="matmul_residual",
    )
    xw_specs = [pl.BlockSpec((bm, k), lambda i, j: (i, 0)), pl.BlockSpec((k, bn), lambda i, j: (0, j))]
    gate_spec = _mod_spec(cfg, layer, gate_k, bm, width=bn, col=True)
    if isinstance(res, tuple):
        pt, r_specs = _split_specs(cfg, bm, bn, True)
        return pl.pallas_call(
            functools.partial(_mr_split_kernel, prompt_tiles=pt),
            in_specs=xw_specs + r_specs + [gate_spec], **common,
        )(x, w, res[0], res[1], mods)
    return pl.pallas_call(
        _mr_kernel,
        in_specs=xw_specs + [pl.BlockSpec((bm, bn), lambda i, j: (i, j)), gate_spec],
        input_output_aliases={2: 0}, **common,
    )(x, w, res, mods)


def _fa_kernel(x_ref, w_ref, cs_ref, o_ref):
    u = jnp.dot(x_ref[...], w_ref[...], preferred_element_type=F32).astype(BF)
    v = jnp.dot(u, cs_ref[...], preferred_element_type=F32)
    n = u.shape[1]
    o_ref[...] = _pack_pair(_bf16_bits(v[:, :n]), _bf16_bits(v[:, n:]))


def _fourier_in(h, w_in, cs):
    t, d = h.shape
    gd = d // FOURIER_GROUPS
    bm = 512
    return pl.pallas_call(
        _fa_kernel,
        grid=(t // bm, FOURIER_GROUPS),
        in_specs=[pl.BlockSpec((bm, d), lambda i, j: (i, 0)),
                  pl.BlockSpec((d, gd), lambda i, j: (0, j)),
                  pl.BlockSpec((gd, 2 * gd), lambda i, j: (0, 0))],
        out_specs=pl.BlockSpec((bm, gd), lambda i, j: (i, j)),
        out_shape=jax.ShapeDtypeStruct((t, d), U32),
        compiler_params=_params(("arbitrary", "arbitrary")),
        name="fourier_in",
    )(h, w_in, cs)


def _fb_kernel(a_ref, v_ref, prev_ref, o_ref):
    del prev_ref
    vc, vs = _unpack_pair(v_ref[...])
    acc = jnp.dot(a_ref[0], vc.astype(BF), preferred_element_type=F32)
    acc = acc + jnp.dot(a_ref[1], vs.astype(BF), preferred_element_type=F32)
    o_ref[...] = acc.astype(o_ref.dtype)


def _fourier_seq_dense(a, v, prev, *, n_seq, seq_len):
    t, d = v.shape
    return pl.pallas_call(
        _fb_kernel,
        grid=(n_seq,),
        in_specs=[pl.BlockSpec((2, seq_len, seq_len), lambda b: (0, 0, 0)),
                  pl.BlockSpec((seq_len, d), lambda b: (b, 0)),
                  pl.BlockSpec(memory_space=pl.ANY)],
        out_specs=pl.BlockSpec((seq_len, d), lambda b: (b, 0)),
        out_shape=jax.ShapeDtypeStruct((t, d), BF),
        input_output_aliases={2: 0},
        compiler_params=_params(("arbitrary",)),
        name="fourier_seq_dense",
    )(a, v, prev)


FFT_RADIX = 8
ROOT_HALF = 0.5 ** 0.5
COS8 = ((1, 1), (1, 2), (0, 0), (-1, 2), (-1, 1), (-1, 2), (0, 0), (1, 2))
SIN8 = ((0, 0), (1, 2), (1, 1), (1, 2), (0, 0), (-1, 2), (-1, 1), (-1, 2))


def _signed_sum(terms):
    acc = None
    for sign, x in terms:
        if acc is None:
            acc = x if sign > 0 else -x
        else:
            acc = acc + x if sign > 0 else acc - x
    return acc


def _fs_kernel(m_ref, twc_ref, tws_ref, va_ref, vb_ref, prev_ref, o_ref, br_ref, bi_ref):
    del prev_ref
    n_sub = va_ref.shape[0] // FFT_RADIX
    for n1 in range(FFT_RADIX):
        halves = [_unpack_pair(v_ref[pl.ds(n1, n_sub, stride=FFT_RADIX), :]) for v_ref in (va_ref, vb_ref)]
        z = jnp.concatenate([jnp.concatenate([h[part].astype(BF) for h in halves], axis=1)
                             for part in range(2)], axis=0)
        a = jnp.dot(m_ref[...], z, preferred_element_type=F32)
        ar, ai = a[:n_sub], a[n_sub:]
        if n1 == 0:
            br_ref[n1], bi_ref[n1] = ar, ai
        else:
            tc, ts = twc_ref[n1], tws_ref[n1]
            br_ref[n1] = tc * ar + ts * ai
            bi_ref[n1] = tc * ai - ts * ar
    for k1 in range(FFT_RADIX):
        unit, root = [], []
        for n1 in range(FFT_RADIX):
            m = (n1 * k1) % FFT_RADIX
            for (sign, cls), ref in ((COS8[m], br_ref), (SIN8[m], bi_ref)):
                if cls:
                    (unit if cls == 1 else root).append((sign, ref[n1]))
        acc = _signed_sum(unit)
        if root:
            acc = acc + ROOT_HALF * _signed_sum(root)
        o_ref[k1 * n_sub:(k1 + 1) * n_sub, :] = acc.astype(o_ref.dtype)


def _fourier_seq_fft(cfg, v, prev):
    t, d = v.shape
    seq = cfg.latent_len
    n_sub = seq // FFT_RADIX
    rb0 = cfg.n_prompt // seq
    cn, sn = _dft_tables(n_sub)
    m = (jnp.block([[cn, -sn], [-sn, -cn]]) * seq ** -0.5).astype(BF)
    ang = (lax.broadcasted_iota(I32, (FFT_RADIX, n_sub), 0) * lax.broadcasted_iota(I32, (FFT_RADIX, n_sub), 1)
           ).astype(F32) * (2.0 * jnp.pi / seq)
    twc = jnp.broadcast_to(jnp.cos(ang)[:, :, None], (FFT_RADIX, n_sub, 2 * LANES))
    tws = jnp.broadcast_to(jnp.sin(ang)[:, :, None], (FFT_RADIX, n_sub, 2 * LANES))
    return pl.pallas_call(
        _fs_kernel,
        grid=(cfg.n_latent // seq, d // (2 * LANES)),
        in_specs=[pl.BlockSpec((2 * n_sub, 2 * n_sub), lambda b, j: (0, 0)),
                  pl.BlockSpec((FFT_RADIX, n_sub, 2 * LANES), lambda b, j: (0, 0, 0)),
                  pl.BlockSpec((FFT_RADIX, n_sub, 2 * LANES), lambda b, j: (0, 0, 0)),
                  pl.BlockSpec((seq, LANES), lambda b, j: (rb0 + b, 2 * j)),
                  pl.BlockSpec((seq, LANES), lambda b, j: (rb0 + b, 2 * j + 1)),
                  pl.BlockSpec(memory_space=pl.ANY)],
        out_specs=pl.BlockSpec((seq, 2 * LANES), lambda b, j: (rb0 + b, j)),
        out_shape=jax.ShapeDtypeStruct((t, d), BF),
        scratch_shapes=[pltpu.VMEM((FFT_RADIX, n_sub, 2 * LANES), F32),
                        pltpu.VMEM((FFT_RADIX, n_sub, 2 * LANES), F32)],
        input_output_aliases={5: 0},
        compiler_params=_params(("arbitrary", "arbitrary")),
        name="fourier_seq_fft",
    )(m, twc, tws, v, v, prev)


def _dft_tables(n):
    j = lax.broadcasted_iota(I32, (n, n), 0)
    k = lax.broadcasted_iota(I32, (n, n), 1)
    ang = ((j * k) % n).astype(F32) * (2.0 * jnp.pi / n)
    return jnp.cos(ang), jnp.sin(ang)


def _fourier_mix(cfg, h, w_in):
    d = cfg.d
    gd = d // FOURIER_GROUPS
    cc, sc = _dft_tables(gd)
    cs = (jnp.concatenate([cc, sc], axis=1) * gd ** -0.5).astype(BF)
    v = _fourier_in(h, w_in, cs)
    cp, sp = _dft_tables(cfg.prompt_len)
    a_p = (jnp.stack([cp, -sp]) * cfg.prompt_len ** -0.5).astype(BF)
    f = _fourier_seq_dense(a_p, v, h, n_seq=cfg.n_prompt // cfg.prompt_len, seq_len=cfg.prompt_len)
    return _fourier_seq_fft(cfg, v, f)


def _softmax_pv(scores, values):
    m = functools.reduce(jnp.maximum, [jnp.max(s, axis=-1, keepdims=True) for s in scores])
    ps = [jnp.exp(s - m) for s in scores]
    l = functools.reduce(jnp.add, [jnp.sum(p, axis=-1, keepdims=True) for p in ps])
    o = functools.reduce(jnp.add, [jnp.dot(p.astype(BF), v, preferred_element_type=F32)
                                   for p, v in zip(ps, values)])
    return o / l


def _qk(q, k):
    return lax.dot_general(q, k, (((1,), (1,)), ((), ())), preferred_element_type=F32)


def _pa_kernel(q_ref, k_ref, v_ref, *rest, heads, dh):
    o_ref = rest[-1]
    for h in range(heads):
        sl = slice(h * dh, (h + 1) * dh)
        s = _qk(q_ref[:, sl], k_ref[:, sl].astype(BF))
        o_ref[:, sl] = _softmax_pv([s], [v_ref[:, sl].astype(BF)]).astype(o_ref.dtype)


def _prompt_attention(cfg, q, k, v, prev):
    lp, d = cfg.prompt_len, cfg.d
    hb = min(8, cfg.heads)
    w = hb * cfg.head_dim
    spec = pl.BlockSpec((lp, w), lambda b, g: (b, g))
    return pl.pallas_call(
        functools.partial(_pa_kernel, heads=hb, dh=cfg.head_dim),
        grid=(cfg.n_prompt // lp, d // w),
        in_specs=[spec, spec, spec, pl.BlockSpec(memory_space=pl.ANY)],
        out_specs=spec,
        out_shape=jax.ShapeDtypeStruct(prev.shape, BF),
        input_output_aliases={3: 0},
        compiler_params=_params(("arbitrary", "arbitrary")),
        name="prompt_attention",
    )(q, k, v, prev)


def _rx_kernel(r_ref, o_ref):
    r = r_ref[...]
    hi = r.astype(BF)
    r1 = r - hi.astype(F32)
    mid = r1.astype(BF)
    lo = (r1 - mid.astype(F32)).astype(BF)
    n = o_ref.shape[1]
    idx = lax.broadcasted_iota(I32, (ROUTE_LANES, n), 1)
    row = lax.broadcasted_iota(I32, (ROUTE_LANES, n), 0)
    qc, kc = idx >> 7, idx & (GRID_W - 1)
    dc = jnp.clip(kc - qc + NA_COLS - 1, 0, 2 * NA_COLS - 2)
    onehot = jnp.where(row == dc, 1.0, 0.0).astype(BF)
    t = (jnp.dot(hi, onehot, preferred_element_type=F32)
         + jnp.dot(mid, onehot, preferred_element_type=F32)
         + jnp.dot(lo, onehot, preferred_element_type=F32))
    col = lax.broadcasted_iota(I32, (1, n), 1)
    qc1, kc1 = col >> 7, col & (GRID_W - 1)
    c0 = jnp.clip(qc1 - NA_COLS // 2, 0, GRID_W - NA_COLS)
    o_ref[...] = jnp.where((kc1 >= c0) & (kc1 < c0 + NA_COLS), t, MASKED)


def _expand_rpb(rpb):
    heads, nr, nc = rpb.shape
    rows = heads * nr
    tr = rows // 4
    r2 = jnp.pad(rpb.reshape(rows, nc), ((0, 0), (0, ROUTE_LANES - nc)))
    out = pl.pallas_call(
        _rx_kernel,
        grid=(rows // tr,),
        in_specs=[pl.BlockSpec((tr, ROUTE_LANES), lambda i: (i, 0))],
        out_specs=pl.BlockSpec((tr, GRID_W * 2 * GRID_W), lambda i: (i, 0)),
        out_shape=jax.ShapeDtypeStruct((rows, GRID_W * 2 * GRID_W), F32),
        compiler_params=_params(("arbitrary",)),
        name="rpb_expand",
    )(r2)
    return out.reshape(heads, nr, GRID_W, 2 * GRID_W)


NA_QROWS = 4
NA_KROWS = NA_QROWS + NA_ROWS
NA_VARIANTS = ((0, 0), (2 * NA_QROWS, NA_QROWS), (GRID_W - NA_QROWS, GRID_W - NA_KROWS))


NA_UNROLL = 8


def _na_kernel(q_ref, k_ref, v_ref, kc_ref, vc_ref, t_ref, prev_ref, o_ref, bias_ref):
    del prev_ref
    w = GRID_W

    @pl.when(pl.program_id(1) == 0)
    def _build_bias():
        left = lax.broadcasted_iota(I32, (w, 2 * w), 1) < w
        masked = jnp.full((w, 2 * w), MASKED, F32)

        def table(r, kr):
            rs = min(max(r - NA_ROWS // 2, 0), w - NA_ROWS)
            return t_ref[0, kr - r + NA_ROWS - 1] if rs <= kr < rs + NA_ROWS else masked

        for var, (r0, k0) in enumerate(NA_VARIANTS):
            for i in range(NA_QROWS):
                for j in range(0, NA_KROWS, 2):
                    pair = jnp.where(left, table(r0 + i, k0 + j), table(r0 + i, k0 + j + 1))
                    bias_ref[var, i * w:(i + 1) * w, j * w:(j + 2) * w] = pair

    kc = kc_ref[0].astype(BF)
    vc = vc_ref[0].astype(BF)
    nq = NA_QROWS * w
    nk = NA_KROWS * w
    n_blocks = w // NA_QROWS

    def block(qb, carry):
        var = jnp.where(qb == 0, 0, jnp.where(qb == n_blocks - 1, 2, 1))
        k0 = jnp.clip(qb * NA_QROWS - NA_ROWS // 2, 0, w - NA_KROWS)
        q0 = pl.multiple_of(qb * nq, nq)
        ks = pl.multiple_of(k0 * w, w)
        q = q_ref[pl.ds(q0, nq), :]
        s_lat = _qk(q, k_ref[pl.ds(ks, nk), :]) + bias_ref[var]
        s_ctx = _qk(q, kc)
        o = _softmax_pv([s_lat, s_ctx], [v_ref[pl.ds(ks, nk), :], vc])
        o_ref[pl.ds(q0, nq), :] = o.astype(o_ref.dtype)
        return carry

    def block_group(g, carry):
        for u in range(NA_UNROLL):
            block(g * NA_UNROLL + u, carry)
        return carry

    lax.fori_loop(0, n_blocks // NA_UNROLL, block_group, 0)


def _latent_attention(cfg, qkv, cache_k, cache_v, t_bias, prev):
    d, dh, heads, ls = cfg.d, cfg.head_dim, cfg.heads, cfg.latent_len
    nb = cfg.n_latent // ls
    rb0 = cfg.n_prompt // ls
    w = GRID_W
    return pl.pallas_call(
        _na_kernel,
        grid=(heads, nb),
        in_specs=[pl.BlockSpec((ls, dh), lambda h, b: (b, h)),
                  pl.BlockSpec((ls, dh), lambda h, b: (b, heads + h)),
                  pl.BlockSpec((ls, dh), lambda h, b: (b, 2 * heads + h)),
                  pl.BlockSpec((1, cfg.past, dh), lambda h, b: (b, 0, h)),
                  pl.BlockSpec((1, cfg.past, dh), lambda h, b: (b, 0, h)),
                  pl.BlockSpec((1, 2 * NA_ROWS - 1, w, 2 * w), lambda h, b: (h, 0, 0, 0)),
                  pl.BlockSpec(memory_space=pl.ANY)],
        out_specs=pl.BlockSpec((ls, dh), lambda h, b: (rb0 + b, h)),
        out_shape=jax.ShapeDtypeStruct(prev.shape, BF),
        scratch_shapes=[pltpu.VMEM((len(NA_VARIANTS), NA_QROWS * w, NA_KROWS * w), F32)],
        input_output_aliases={6: 0},
        compiler_params=_params(("arbitrary", "arbitrary")),
        name="latent_attention",
    )(qkv, qkv, qkv, cache_k, cache_v, t_bias, prev)


U32 = jnp.uint32
HIGH_HALF = 0xFFFF0000


def _slab(d):
    rows = d // (2 * LANES)
    return rows, -(-rows // SUBLANES) * SUBLANES + SLAB_PAD


def _bf16_bits(x):
    return lax.bitcast_convert_type(x.astype(BF).astype(F32), U32)


def _pack_pair(lo_bits, hi_bits):
    return (hi_bits & jnp.uint32(HIGH_HALF)) | (lo_bits >> 16)


def _unpack_pair(w):
    return (lax.bitcast_convert_type(w << 16, F32),
            lax.bitcast_convert_type(w & jnp.uint32(HIGH_HALF), F32))


def _store_slabs(ref, bits):
    n, d = bits.shape
    rows, pitch = _slab(d)
    for r in range(rows):
        lo = bits[:, r * LANES:(r + 1) * LANES]
        hi = bits[:, d // 2 + r * LANES:d // 2 + (r + 1) * LANES]
        ref[pl.ds(r, n, stride=pitch), :] = _pack_pair(lo, hi)
    for r in range(rows, pitch):
        ref[pl.ds(r, n, stride=pitch), :] = jnp.zeros((n, LANES), U32)


def _load_slab_piece(ref, lead, n, pitch, r):
    return _unpack_pair(ref[lead + (pl.ds(r, n, stride=pitch), slice(None))])


def _nmr_kernel(x_ref, g_ref, sc_ref, sh_ref, wr_ref, br_ref, hs_ref, ri_ref, rg_ref, cnt_ref, base_ref):
    tm, d = x_ref.shape

    @pl.when(pl.program_id(0) == 0)
    def _init():
        base_ref[...] = jnp.zeros_like(base_ref)

    h = _norm_mod(x_ref[...], g_ref[...], sc_ref[0], sh_ref[0])
    h_hi = h.astype(BF)
    h_hi32 = h_hi.astype(F32)
    _store_slabs(hs_ref, lax.bitcast_convert_type(h_hi32, U32))

    h_lo = (h - h_hi32).astype(BF)
    wr = wr_ref[...]
    w_hi = wr.astype(BF)
    w_lo = (wr - w_hi.astype(F32)).astype(BF)
    logits = (jnp.dot(h_hi, w_hi, preferred_element_type=F32)
              + jnp.dot(h_lo, w_hi, preferred_element_type=F32)
              + jnp.dot(h_hi, w_lo, preferred_element_type=F32)) + br_ref[...]

    lane = lax.broadcasted_iota(I32, (tm, ROUTE_LANES), 1)
    lane_f = lane.astype(F32)

    def first_argmax(v):
        m = jnp.max(v, axis=-1, keepdims=True)
        first = jnp.min(jnp.where(v == m, lane_f, float(ROUTE_LANES)), axis=-1, keepdims=True)
        return m, first.astype(I32)

    gl = jnp.where(lane < N_EXPERT_GROUPS, logits, MASKED)
    gmax, gi = first_argmax(gl)
    g_w = 1.0 / jnp.sum(jnp.exp(gl - gmax), axis=-1, keepdims=True)
    e0 = N_EXPERT_GROUPS + gi * EXPERTS_PER_GROUP
    el = jnp.where((lane >= e0) & (lane < e0 + EXPERTS_PER_GROUP), logits, MASKED)
    m1, i1 = first_argmax(el)
    m2, i2 = first_argmax(jnp.where(lane == i1, MASKED, el))
    tt = jnp.exp(m2 - m1)
    w1 = 1.0 / (1.0 + tt)
    eid0, eid1 = i1 - N_EXPERT_GROUPS, i2 - N_EXPERT_GROUPS

    oh = jnp.where((lane == eid0) | (lane == eid1), 1.0, 0.0)
    row = lax.broadcasted_iota(I32, (tm, tm), 0)
    colt = lax.broadcasted_iota(I32, (tm, tm), 1)
    earlier = jnp.where(row > colt, 1.0, 0.0).astype(BF)
    before = jnp.dot(earlier, oh.astype(BF), preferred_element_type=F32) + base_ref[0:1, :]
    rank0 = jnp.sum(jnp.where(lane == eid0, before, 0.0), axis=-1, keepdims=True).astype(I32)
    rank1 = jnp.sum(jnp.where(lane == eid1, before, 0.0), axis=-1, keepdims=True).astype(I32)
    base_ref[...] = base_ref[...] + jnp.sum(oh, axis=0, keepdims=True)

    ri_ref[...] = jnp.where(lane == 0, eid0, jnp.where(lane == 1, eid1,
                            jnp.where(lane == 2, rank0, jnp.where(lane == 3, rank1, 0))))
    rg_ref[...] = jnp.where(lane == 0, g_w * w1, jnp.where(lane == 1, g_w * (tt * w1), 0.0))
    cnt_ref[...] = base_ref[...]


def _norm_route(cfg, x, g, mods, layer, w_route, b_route):
    t, d = x.shape
    tm = 256
    _, pitch = _slab(d)
    return pl.pallas_call(
        _nmr_kernel,
        grid=(t // tm,),
        in_specs=[pl.BlockSpec((tm, d), lambda i: (i, 0)),
                  pl.BlockSpec((1, d), lambda i: (0, 0)),
                  _mod_spec(cfg, layer, 4, tm),
                  _mod_spec(cfg, layer, 3, tm),
                  pl.BlockSpec((d, ROUTE_LANES), lambda i: (0, 0)),
                  pl.BlockSpec((1, ROUTE_LANES), lambda i: (0, 0))],
        out_specs=[pl.BlockSpec((tm * pitch, LANES), lambda i: (i, 0)),
                   pl.BlockSpec((tm, ROUTE_LANES), lambda i: (i, 0)),
                   pl.BlockSpec((tm, ROUTE_LANES), lambda i: (i, 0)),
                   pl.BlockSpec((SUBLANES, ROUTE_LANES), lambda i: (0, 0))],
        out_shape=[jax.ShapeDtypeStruct((t * pitch, LANES), U32),
                   jax.ShapeDtypeStruct((t, ROUTE_LANES), I32),
                   jax.ShapeDtypeStruct((t, ROUTE_LANES), F32),
                   jax.ShapeDtypeStruct((SUBLANES, ROUTE_LANES), F32)],
        scratch_shapes=[pltpu.VMEM((SUBLANES, ROUTE_LANES), F32)],
        compiler_params=_params(("arbitrary",)),
        name="norm_route",
    )(x, g.reshape(1, d), mods, mods, w_route, b_route)


MOVE_TOKENS = 256
SCATTER_TOKENS = 512


def _wait_bytes_of(buf_ref, sem):
    pltpu.make_async_copy(buf_ref, buf_ref, sem).wait()


def _ds_kernel(dest_ref, zstart_ref, hs_ref, xs_ref, zbuf_ref, sem, zsem):
    n = SCATTER_TOKENS
    pitch = hs_ref.shape[0] // n
    bm = zbuf_ref.shape[0]
    n_blocks = xs_ref.shape[0] // bm

    @pl.when(pl.program_id(0) == 0)
    def _zero_fill():
        zbuf_ref[...] = jnp.zeros_like(zbuf_ref)
        n_used = zstart_ref[0, 0, N_EXPERTS]
        targets = [(zstart_ref[0, 0, e] >= 0, zstart_ref[0, 0, e]) for e in range(N_EXPERTS)]
        targets += [(n_used + j < n_blocks, (n_used + j) * bm) for j in range(N_EXPERTS)]
        for wait in (False, True):
            for live, start in targets:
                @pl.when(live)
                def _():
                    copy = pltpu.make_async_copy(zbuf_ref, xs_ref.at[pl.ds(start, bm)], zsem)
                    copy.wait() if wait else copy.start()

    def issue(tk, carry):
        src = hs_ref.at[pl.ds(pl.multiple_of(tk * pitch, SUBLANES), pitch), :]
        for k in range(2):
            pltpu.make_async_copy(src, xs_ref.at[dest_ref[0, 0, 2 * tk + k]], sem).start()
        return carry

    lax.fori_loop(0, n, issue, 0)
    _wait_bytes_of(xs_ref.at[pl.ds(0, 2 * n)], sem)


def _dispatch(hs, dest, zero_plan, n_slots, bm):
    n = SCATTER_TOKENS
    t = dest.shape[0] // 2
    pitch = hs.shape[0] // t
    return pl.pallas_call(
        _ds_kernel,
        grid=(t // n,),
        in_specs=[pl.BlockSpec((1, 1, 2 * n), lambda i: (i, 0, 0), memory_space=pltpu.SMEM),
                  pl.BlockSpec((1, 1, N_EXPERTS + 1), lambda i: (0, 0, 0), memory_space=pltpu.SMEM),
                  pl.BlockSpec((n * pitch, LANES), lambda i: (i, 0))],
        out_specs=pl.BlockSpec(memory_space=pl.ANY),
        out_shape=jax.ShapeDtypeStruct((n_slots, pitch, LANES), U32),
        scratch_shapes=[pltpu.VMEM((bm, pitch, LANES), U32), pltpu.SemaphoreType.DMA, pltpu.SemaphoreType.DMA],
        compiler_params=_params(("arbitrary",)),
        name="moe_dispatch",
    )(dest.reshape(t // n, 1, 2 * n), zero_plan.reshape(1, 1, N_EXPERTS + 1), hs)


def _ex_kernel(blk_e_ref, n_used_ref, x_ref, wg_ref, wu_ref, wd_ref, o_ref, xb_ref):
    del blk_e_ref
    bm, d = xb_ref.shape
    rows, pitch = _slab(d)

    @pl.when(pl.program_id(0) < n_used_ref[0])
    def _compute():
        for r in range(rows):
            lo, hi = _load_slab_piece(x_ref, (), bm, pitch, r)
            xb_ref[:, r * LANES:(r + 1) * LANES] = lo.astype(BF)
            xb_ref[:, d // 2 + r * LANES:d // 2 + (r + 1) * LANES] = hi.astype(BF)
        x = xb_ref[...]
        g = jnp.dot(x, wg_ref[0], preferred_element_type=F32)
        u = jnp.dot(x, wu_ref[0], preferred_element_type=F32)
        y = jnp.dot((_silu(g) * u).astype(BF), wd_ref[0], preferred_element_type=F32)
        _store_slabs(o_ref, _bf16_bits(y))


def _experts(cfg, xs, blk_e, n_used, wg, wu, wd):
    n_slots, sr, sw = xs.shape
    d, ff, bm = cfg.d, cfg.ff, cfg.moe_block
    nb = n_slots // bm

    def blk(b, be, nu):
        return jnp.minimum(b, nu[0] - 1)

    grid_spec = pltpu.PrefetchScalarGridSpec(
        num_scalar_prefetch=2,
        grid=(nb,),
        in_specs=[pl.BlockSpec((bm * sr, sw), lambda b, be, nu: (blk(b, be, nu), 0)),
                  pl.BlockSpec((1, d, ff), lambda b, be, nu: (be[blk(b, be, nu)], 0, 0)),
                  pl.BlockSpec((1, d, ff), lambda b, be, nu: (be[blk(b, be, nu)], 0, 0)),
                  pl.BlockSpec((1, ff, d), lambda b, be, nu: (be[blk(b, be, nu)], 0, 0))],
        out_specs=pl.BlockSpec((bm * sr, sw), lambda b, be, nu: (blk(b, be, nu), 0)),
        scratch_shapes=[pltpu.VMEM((bm, d), BF)],
    )
    ys = pl.pallas_call(
        _ex_kernel,
        grid_spec=grid_spec,
        out_shape=jax.ShapeDtypeStruct((n_slots * sr, sw), U32),
        input_output_aliases={2: 0},
        compiler_params=_params(("arbitrary",)),
        name="moe_experts",
    )(blk_e, n_used, xs.reshape(n_slots * sr, sw), wg, wu, wd)
    return ys.reshape(n_slots, sr, sw)


def _gather_combine(dcur_ref, dnext_ref, x_ref, ys_ref, rg_ref, gt_ref, o_ref, ybuf_ref, sems):
    tm, d = x_ref.shape
    rows, pitch = _slab(d)
    copy_rows = pitch - SLAB_PAD
    i, n_steps = pl.program_id(0), pl.num_programs(0)

    def start_gather(dref, slot):
        def issue(tk, carry):
            for k in range(2):
                row0 = pl.multiple_of((k * tm + tk) * pitch, SUBLANES)
                pltpu.make_async_copy(ys_ref.at[dref[0, 0, 2 * tk + k], pl.ds(0, copy_rows), :],
                                      ybuf_ref.at[slot, pl.ds(row0, copy_rows), :], sems.at[slot]).start()
            return carry
        lax.fori_loop(0, tm, issue, 0)

    @pl.when(i == 0)
    def _first():
        start_gather(dcur_ref, 0)

    @pl.when(i + 1 < n_steps)
    def _next():
        start_gather(dnext_ref, (i + 1) % 2)

    slot = i % 2
    _wait_bytes_of(ybuf_ref.at[slot, pl.ds(0, 2 * tm * copy_rows), :], sems.at[slot])
    g0, g1 = rg_ref[:, 0:1], rg_ref[:, 1:2]
    for r in range(rows):
        lo0, hi0 = _load_slab_piece(ybuf_ref, (slot,), tm, pitch, r)
        lo1, hi1 = _load_slab_piece(ybuf_ref, (slot,), tm, pitch, tm * pitch + r)
        for c0, y0, y1 in ((r * LANES, lo0, lo1), (d // 2 + r * LANES, hi0, hi1)):
            sl = slice(c0, c0 + LANES)
            o_ref[:, sl] = x_ref[:, sl] + gt_ref[0][:, sl] * (y0 * g0 + y1 * g1)


def _cb_next_kernel(dcur_ref, dnext_ref, x_ref, ys_ref, rg_ref, gt_ref, g_ref, sc_ref, sh_ref,
                    o_ref, hn_ref, ybuf_ref, sems):
    _gather_combine(dcur_ref, dnext_ref, x_ref, ys_ref, rg_ref, gt_ref, o_ref, ybuf_ref, sems)
    hn_ref[...] = _norm_mod(o_ref[...], g_ref[...], sc_ref[0], sh_ref[0]).astype(hn_ref.dtype)


def _cb_final_kernel(dcur_ref, dnext_ref, x_ref, ys_ref, rg_ref, gt_ref, g_ref,
                     yp_ref, yl_ref, ybuf_ref, sems, xn_ref, *, prompt_tiles):
    _gather_combine(dcur_ref, dnext_ref, x_ref, ys_ref, rg_ref, gt_ref, xn_ref, ybuf_ref, sems)
    xn = xn_ref[...]
    y = xn * lax.rsqrt(jnp.mean(xn * xn, axis=-1, keepdims=True) + NORM_EPS) * g_ref[...]
    i = pl.program_id(0)

    @pl.when(i < prompt_tiles)
    def _prompt():
        yp_ref[...] = y

    @pl.when(i >= prompt_tiles)
    def _latent():
        yl_ref[...] = y


def _combine(cfg, x, ys, dest, rg, mods, layer, g, next_layer):
    t, d = x.shape
    tm = MOVE_TOKENS
    _, pitch = _slab(d)
    n = t // tm
    dest3 = dest.reshape(n, 1, 2 * tm)
    row = pl.BlockSpec((tm, d), lambda i: (i, 0))
    in_specs = [pl.BlockSpec((1, 1, 2 * tm), lambda i: (i, 0, 0), memory_space=pltpu.SMEM),
                pl.BlockSpec((1, 1, 2 * tm), lambda i: (jnp.minimum(i + 1, n - 1), 0, 0),
                             memory_space=pltpu.SMEM),
                row,
                pl.BlockSpec(memory_space=pl.ANY),
                pl.BlockSpec((tm, ROUTE_LANES), lambda i: (i, 0)),
                _mod_spec(cfg, layer, 5, tm),
                pl.BlockSpec((1, d), lambda i: (0, 0))]
    scratch = [pltpu.VMEM((2, 2 * tm * pitch, LANES), U32), pltpu.SemaphoreType.DMA((2,))]
    args = [dest3, dest3, x, ys, rg, mods, g.reshape(1, d)]
    if next_layer is not None:
        return pl.pallas_call(
            _cb_next_kernel,
            grid=(n,),
            in_specs=in_specs + [_mod_spec(cfg, next_layer, 1, tm), _mod_spec(cfg, next_layer, 0, tm)],
            out_specs=[row, row],
            out_shape=[jax.ShapeDtypeStruct((t, d), F32), jax.ShapeDtypeStruct((t, d), BF)],
            scratch_shapes=scratch,
            input_output_aliases={2: 0},
            compiler_params=_params(("arbitrary",)),
            name="moe_combine_norm",
        )(*args, mods, mods)
    pt = cfg.n_prompt // tm
    return pl.pallas_call(
        functools.partial(_cb_final_kernel, prompt_tiles=pt),
        grid=(n,),
        in_specs=in_specs,
        out_specs=[pl.BlockSpec((tm, d), lambda i: (jnp.minimum(i, pt - 1), 0)),
                   pl.BlockSpec((tm, d), lambda i: (jnp.maximum(i - pt, 0), 0))],
        out_shape=[jax.ShapeDtypeStruct((cfg.n_prompt, d), F32), jax.ShapeDtypeStruct((cfg.n_latent, d), F32)],
        scratch_shapes=scratch + [pltpu.VMEM((tm, d), F32)],
        compiler_params=_params(("arbitrary",)),
        name="moe_combine_final",
    )(*args)


def _moe(cfg, x, g, mods, layer, w_grp, b_grp, w_exp, b_exp, wg, wu, wd, g_after, next_layer):
    t, d = x.shape
    bm = cfg.moe_block
    pad = ROUTE_LANES - N_EXPERT_GROUPS - N_EXPERTS
    w_route = jnp.pad(jnp.concatenate([w_grp, w_exp], axis=1), ((0, 0), (0, pad)))
    b_route = jnp.pad(jnp.concatenate([b_grp, b_exp]), (0, pad)).reshape(1, ROUTE_LANES)
    hs, ri, rg, cnt = _norm_route(cfg, x, g, mods, layer, w_route, b_route)

    counts = cnt[0, :N_EXPERTS].astype(I32)
    padded = (counts + bm - 1) // bm * bm
    ends = jnp.cumsum(padded)
    starts = ends - padded
    eid, rank = ri[:, 0:2], ri[:, 2:4]
    start_of = jnp.sum(jnp.where(eid[:, :, None] == jnp.arange(N_EXPERTS, dtype=I32), starts, 0), axis=-1)
    dest = (start_of + rank).reshape(-1)
    n_blocks = -(-2 * t // bm) + N_EXPERTS
    blk_e = jnp.minimum(jnp.sum(ends[None, :] <= (jnp.arange(n_blocks, dtype=I32) * bm)[:, None], axis=1),
                        N_EXPERTS - 1).astype(I32) + layer * N_EXPERTS
    n_used = (ends[-1:] // bm).astype(I32)
    zero_plan = jnp.concatenate([jnp.where(counts > 0, ends - bm, -1).astype(I32), n_used])

    xs = _dispatch(hs, dest, zero_plan, n_blocks * bm, bm)
    ys = _experts(cfg, xs, blk_e, n_used, wg, wu, wd)
    return _combine(cfg, x, ys, dest, rg, mods, layer, g_after, next_layer)


def _forward(cfg, x_prompt, x_sample, cache_k, cache_v, c, c_ctx, w_ada, b_ada, norm_mix_g, norm_ffn_g,
             fourier_w_in, fourier_w_out, na_w_qkv, na_w_out, na_rpb,
             router_grp_w, router_grp_b, router_exp_w, router_exp_b,
             expert_w_gate, expert_w_up, expert_w_down, final_norm_g):
    d = cfg.d
    depth = w_ada.shape[0]
    xp, xl = x_prompt.reshape(cfg.n_prompt, d), x_sample.reshape(cfg.n_latent, d)

    nb_lat = cfg.n_latent // cfg.latent_len
    cond = jnp.concatenate([c_ctx[None], c, jnp.zeros((COND_ROWS - 1 - nb_lat, d), F32)], axis=0)
    mods = _ada(cond, w_ada, b_ada).reshape(depth * COND_ROWS * N_MODS, 1, d)

    n_all = depth * N_EXPERTS
    wg_all = expert_w_gate.astype(BF).reshape(n_all, d, cfg.ff)
    wu_all = expert_w_up.astype(BF).reshape(n_all, d, cfg.ff)
    wd_all = expert_w_down.astype(BF).reshape(n_all, cfg.ff, d)

    x = (xp, xl)
    h = _nm(cfg, xp, xl, norm_mix_g[0], mods, 0, 1, 0)
    new_k, new_v = [], []
    for i in range(depth):
        j = i // 2
        if i % 2 == 0:
            f = _fourier_mix(cfg, h, fourier_w_in[j].astype(BF))
            x = _mm_residual(cfg, f, fourier_w_out[j].astype(BF), x, mods, i, 2)
        else:
            w_qkv = na_w_qkv[j].astype(BF)
            scale = cfg.head_dim ** -0.5
            q_p = _mm(h, w_qkv, row0=0, rows=cfg.n_prompt, col0=0, cols=d, out_dtype=BF,
                      scaled_cols=d, scale=scale)
            k_p = _mm(h, w_qkv, row0=0, rows=cfg.n_prompt, col0=d, cols=d, out_dtype=F32)
            v_p = _mm(h, w_qkv, row0=0, rows=cfg.n_prompt, col0=2 * d, cols=d, out_dtype=F32)
            qkv_l = _mm(h, w_qkv, row0=cfg.n_prompt, rows=cfg.n_latent, col0=0, cols=3 * d, out_dtype=BF,
                        scaled_cols=d, scale=scale)
            new_k.append(k_p)
            new_v.append(v_p)
            o = _prompt_attention(cfg, q_p, k_p, v_p, h)
            t_bias = _expand_rpb(na_rpb[j])
            ck = cache_k[:, j].reshape(nb_lat, cfg.past, d)
            cv = cache_v[:, j].reshape(nb_lat, cfg.past, d)
            o = _latent_attention(cfg, qkv_l, ck, cv, t_bias, o)
            x = _mm_residual(cfg, o, na_w_out[j].astype(BF), x, mods, i, 2)
        last = i == depth - 1
        x, h = _moe(cfg, x, norm_ffn_g[i], mods, i, router_grp_w[i], router_grp_b[i], router_exp_w[i],
                    router_exp_b[i], wg_all, wu_all, wd_all,
                    final_norm_g if last else norm_mix_g[i + 1], None if last else i + 1)

    y_prompt, y_sample = x.reshape(x_prompt.shape), h.reshape(x_sample.shape)
    nbp = x_prompt.shape[0]
    kv_shape = (nbp, cfg.prompt_len, cfg.heads, cfg.head_dim)
    new_k = jnp.stack([a.reshape(kv_shape) for a in new_k], axis=1)
    new_v = jnp.stack([a.reshape(kv_shape) for a in new_v], axis=1)
    return (y_prompt, y_sample, new_k, new_v)


def kernel(x_prompt, x_sample, cache_k, cache_v, c, c_ctx, w_ada, b_ada, norm_mix_g, norm_ffn_g,
           fourier_w_in, fourier_w_out, na_w_qkv, na_w_out, na_rpb,
           router_grp_w, router_grp_b, router_exp_w, router_exp_b,
           expert_w_gate, expert_w_up, expert_w_down, final_norm_g):
    bp, lp, d = x_prompt.shape
    bl, ll, _ = x_sample.shape
    heads = na_rpb.shape[1]
    cfg = Cfg(d=d, n_prompt=bp * lp, prompt_len=lp, n_latent=bl * ll, latent_len=ll, heads=heads,
              head_dim=d // heads, past=cache_k.shape[2], ff=expert_w_gate.shape[-1], moe_block=256)
    return _forward(cfg, x_prompt, x_sample, cache_k, cache_v, c, c_ctx, w_ada, b_ada, norm_mix_g,
                    norm_ffn_g, fourier_w_in, fourier_w_out, na_w_qkv, na_w_out, na_rpb,
                    router_grp_w, router_grp_b, router_exp_w, router_exp_b,
                    expert_w_gate, expert_w_up, expert_w_down, final_norm_g)
```
